```python
import jax, jax.numpy as jnp
from jax import lax
import numpy as np

D_MODEL = 1024
BATCH = 8
SEQ = 4096
DEPTH = 4

CHUNK = 64
Q_BLOCK = 2 * CHUNK
HEAD_DIM = 64
SB_WIDTH = D_MODEL // 2
SB_HEADS = SB_WIDTH // HEAD_DIM
CONV_WIDTH = D_MODEL // 4
CONV_K = 3
POOL_WINDOWS = (2, 4, 8, 16)
POOL_GROUPS = len(POOL_WINDOWS)
POOL_WIDTH = D_MODEL // 4
POOL_GDIM = POOL_WIDTH // POOL_GROUPS
MIX_WIDTH = SB_WIDTH + CONV_WIDTH + POOL_WIDTH
IN_WIDTH = 3 * SB_WIDTH + 3 * CONV_WIDTH + POOL_WIDTH
D_FF = 2816
N_MOD = 9
EPS = 1e-6

kernel_name = "hybrid_sb_conv_pool_macaron_adaln"


def rms_norm(x, gain):
    x32 = x.astype(jnp.float32)
    y = x32 * lax.rsqrt(jnp.mean(x32 * x32, axis=-1, keepdims=True) + EPS)
    return y.astype(x.dtype) * gain.astype(x.dtype)


def modulate(h, shift, scale):
    return h * (1 + scale[:, None, :]) + shift[:, None, :]


def swiglu(h, w_gate, w_up, w_down):
    return (jax.nn.silu(h @ w_gate) * (h @ w_up)) @ w_down


def stick_breaking_attention(q, k, v):
    seq = q.shape[2]
    scale = q.shape[-1] ** -0.5
    outs = []
    for i in range(seq // Q_BLOCK):
        q0 = i * Q_BLOCK
        kv_len = q0 + Q_BLOCK
        qb = q[:, :, q0:kv_len]
        kb = k[:, :, :kv_len]
        vb = v[:, :, :kv_len]
        z = jnp.einsum('bhqd,bhkd->bhqk', qb, kb).astype(jnp.float32) * scale
        t_pos = q0 + jnp.arange(Q_BLOCK)[:, None]
        s_pos = jnp.arange(kv_len)[None, :]
        past = s_pos < t_pos
        log_keep = jnp.where(past, -jax.nn.softplus(z), 0.0)
        suffix = lax.cumsum(log_keep, axis=3, reverse=True) - log_keep
        log_a = jax.nn.log_sigmoid(z) + suffix
        a = jnp.where(past, jnp.exp(log_a), 0.0)
        outs.append(jnp.einsum('bhqk,bhkd->bhqd', a.astype(vb.dtype), vb))
    return jnp.concatenate(outs, axis=2)


def short_conv_mixer(u, gate_b, gate_c, conv_w):
    cu = gate_c * u
    y = lax.conv_general_dilated(
        cu, conv_w[:, None, :].astype(cu.dtype), window_strides=(1,),
        padding=[(CONV_K - 1, 0)], dimension_numbers=('NWC', 'WIO', 'NWC'),
        feature_group_count=cu.shape[-1])
    return gate_b * y


def multiscale_pool_mixer(p, pool_w, pool_scale):
    bsz, seq, _ = p.shape
    pg = p.reshape(bsz, seq, POOL_GROUPS, POOL_GDIM)
    cs = jnp.cumsum(pg.astype(jnp.float32), axis=1)
    t1 = jnp.arange(1, seq + 1, dtype=jnp.float32)
    pooled = []
    for g, w in enumerate(POOL_WINDOWS):
        c_g = cs[:, :, g]
        lagged = jnp.pad(c_g, ((0, 0), (w, 0), (0, 0)))[:, :seq]
        count = jnp.minimum(t1, float(w))[None, :, None]
        pooled.append((c_g - lagged) / count)
    pooled = jnp.stack(pooled, axis=2).astype(p.dtype) - pg
    y = jnp.einsum('bsgc,gcd->bsgd', pooled, pool_w).reshape(bsz, seq, POOL_WIDTH)
    return y * pool_scale


def setup_inputs(seed: int = 0) -> dict:
    key = jax.random.key(seed)
    ks = jax.random.split(key, 24)
    f32 = jnp.float32
    nrm = lambda k, shape, s: jax.random.normal(k, shape, f32) * s
    gain = lambda k, shape: 1.0 + 0.02 * jax.random.normal(k, shape, f32)
    L = DEPTH
    return {
        "x": jax.random.normal(ks[0], (BATCH, SEQ, D_MODEL), f32),
        "c": jax.random.normal(ks[1], (BATCH, D_MODEL), f32),
        "w_ada": nrm(ks[2], (L, D_MODEL, N_MOD * D_MODEL), 0.1 * D_MODEL ** -0.5),
        "b_ada": nrm(ks[3], (L, N_MOD * D_MODEL), 0.01),
        "ffn1_norm": gain(ks[4], (L, D_MODEL)),
        "ffn1_gate": nrm(ks[5], (L, D_MODEL, D_FF), D_MODEL ** -0.5),
        "ffn1_up": nrm(ks[6], (L, D_MODEL, D_FF), D_MODEL ** -0.5),
        "ffn1_down": nrm(ks[7], (L, D_FF, D_MODEL), D_FF ** -0.5),
        "mix_norm": gain(ks[8], (L, D_MODEL)),
        "w_in": nrm(ks[9], (L, D_MODEL, IN_WIDTH), D_MODEL ** -0.5),
        "q_norm": gain(ks[10], (L, HEAD_DIM)),
        "k_norm": gain(ks[11], (L, HEAD_DIM)),
        "conv_w": nrm(ks[12], (L, CONV_K, CONV_WIDTH), CONV_K ** -0.5),
        "pool_w": nrm(ks[13], (L, POOL_GROUPS, POOL_GDIM, POOL_GDIM), POOL_GDIM ** -0.5),
        "pool_scale": gain(ks[14], (L, POOL_WIDTH)),
        "w_out": nrm(ks[15], (L, MIX_WIDTH, D_MODEL), MIX_WIDTH ** -0.5),
        "ffn2_norm": gain(ks[16], (L, D_MODEL)),
        "ffn2_gate": nrm(ks[17], (L, D_MODEL, D_FF), D_MODEL ** -0.5),
        "ffn2_up": nrm(ks[18], (L, D_MODEL, D_FF), D_MODEL ** -0.5),
        "ffn2_down": nrm(ks[19], (L, D_FF, D_MODEL), D_FF ** -0.5),
    }


def reference(x, c, w_ada, b_ada, ffn1_norm, ffn1_gate, ffn1_up, ffn1_down,
              mix_norm, w_in, q_norm, k_norm, conv_w, pool_w, pool_scale, w_out,
              ffn2_norm, ffn2_gate, ffn2_up, ffn2_down):
    bsz, seq, _ = x.shape
    splits = [SB_WIDTH, 2 * SB_WIDTH, 3 * SB_WIDTH,
              3 * SB_WIDTH + CONV_WIDTH, 3 * SB_WIDTH + 2 * CONV_WIDTH,
              3 * SB_WIDTH + 3 * CONV_WIDTH]
    cond = jax.nn.silu(c)

    def to_heads(t):
        return t.reshape(bsz, seq, SB_HEADS, HEAD_DIM).transpose(0, 2, 1, 3)

    for l in range(DEPTH):
        mod = cond @ w_ada[l] + b_ada[l]
        sh1, sc1, g1, sh2, sc2, g2, sh3, sc3, g3 = jnp.split(mod, N_MOD, axis=-1)

        h = modulate(rms_norm(x, ffn1_norm[l]), sh1, sc1)
        x = x + 0.5 * (1 + g1)[:, None, :] * swiglu(h, ffn1_gate[l], ffn1_up[l], ffn1_down[l])

        h = modulate(rms_norm(x, mix_norm[l]), sh2, sc2)
        proj = h @ w_in[l]
        q, k, v, cb, cc, cu, p = jnp.split(proj, splits, axis=-1)
        q = rms_norm(to_heads(q), q_norm[l])
        k = rms_norm(to_heads(k), k_norm[l])
        y_sb = stick_breaking_attention(q, k, to_heads(v))
        y_sb = y_sb.transpose(0, 2, 1, 3).reshape(bsz, seq, SB_WIDTH)
        y_conv = short_conv_mixer(cu, cb, cc, conv_w[l])
        y_pool = multiscale_pool_mixer(p, pool_w[l], pool_scale[l])
        mixed = jnp.concatenate([y_sb, y_conv, y_pool], axis=-1) @ w_out[l]
        x = x + (1 + g2)[:, None, :] * mixed

        h = modulate(rms_norm(x, ffn2_norm[l]), sh3, sc3)
        x = x + 0.5 * (1 + g3)[:, None, :] * swiglu(h, ffn2_gate[l], ffn2_up[l], ffn2_down[l])
    return x
```

```python
import functools
import math

import jax
import jax.numpy as jnp
from jax import lax
from jax.experimental import pallas as pl
from jax.experimental.pallas import tpu as pltpu

F32 = jnp.float32
BF16 = jnp.bfloat16

D_MODEL = 1024
HEAD_DIM = 64
SB_WIDTH = 512
CONV_WIDTH = 256
CONV_K = 3
POOL_WINDOWS = (2, 4, 8, 16)
POOL_WIDTH = 256
IN_WIDTH = 3 * SB_WIDTH + 3 * CONV_WIDTH + POOL_WIDTH
D_FF = 2816
N_MOD = 9
EPS = 1e-6

LANES = 128
SUBLANES = 8
VMEM_LIMIT_BYTES = 56 * 1024 * 1024

ADA_TN = 1536
ROW_TILE = 512
FF_CHUNK = 256
N_FF_CHUNKS = D_FF // FF_CHUNK
ATT_TILE = 256
POOL_HALO = max(POOL_WINDOWS)
HIST_PAD = SUBLANES
HIST_OFF = HIST_PAD + POOL_HALO

LOG2E = math.log2(math.e)
LN2 = math.log(2.0)
Z_CLAMP = 126.0


def _params(n_grid):
    return pltpu.CompilerParams(
        dimension_semantics=("arbitrary",) * n_grid,
        vmem_limit_bytes=VMEM_LIMIT_BYTES)


def _resident(shape):
    zeros = (0,) * len(shape)
    return pl.BlockSpec(shape, lambda *_: zeros, pipeline_mode=pl.Buffered(1))


def _ada_kernel(c_ref, w_ref, b_ref, o_ref):
    c = c_ref[...]
    cond = (c * jax.nn.sigmoid(c)).astype(BF16)
    w = w_ref[0].astype(BF16)
    o_ref[0] = jnp.dot(cond, w, preferred_element_type=F32) + b_ref[0]


def _ada_all_layers(c, w_ada, b_ada):
    n_layers, _, n_out = w_ada.shape
    bsz = c.shape[0]
    return pl.pallas_call(
        _ada_kernel,
        grid=(n_layers, n_out // ADA_TN),
        in_specs=[
            pl.BlockSpec((bsz, D_MODEL), lambda l, j: (0, 0)),
            pl.BlockSpec((1, D_MODEL, ADA_TN), lambda l, j: (l, 0, j)),
            pl.BlockSpec((1, 1, ADA_TN), lambda l, j: (l, 0, j)),
        ],
        out_specs=pl.BlockSpec((1, bsz, ADA_TN), lambda l, j: (l, 0, j)),
        out_shape=jax.ShapeDtypeStruct((n_layers, bsz, n_out), F32),
        compiler_params=_params(2),
        name="adaln",
    )(c, w_ada, b_ada.reshape(n_layers, 1, n_out))


def _norm_mod(x, gain, shift, scale):
    ms = jnp.mean(x * x, axis=-1, keepdims=True)
    y = x * lax.rsqrt(ms + EPS) * gain
    return y * (1.0 + scale) + shift


def _ffn_kernel(x_ref, mod_ref, gain_ref, wg_ref, wu_ref, wd_ref, o_ref, h_ref, acc_ref, *, mod_base):
    x = x_ref[0]
    m = mod_ref[0]
    shift = m[mod_base:mod_base + 1]
    scale = m[mod_base + 1:mod_base + 2]
    gate = m[mod_base + 2:mod_base + 3]
    h_ref[...] = _norm_mod(x, gain_ref[...], shift, scale).astype(BF16)
    acc_ref[...] = jnp.zeros_like(acc_ref)

    def chunk(c, carry):
        h = h_ref[...]
        g = jnp.dot(h, wg_ref[c], preferred_element_type=F32)
        u = jnp.dot(h, wu_ref[c], preferred_element_type=F32)
        a = (g * jax.nn.sigmoid(g) * u).astype(BF16)
        acc_ref[...] += jnp.dot(a, wd_ref[c], preferred_element_type=F32)
        return carry

    lax.fori_loop(0, N_FF_CHUNKS, chunk, 0)
    o_ref[0] = x + (0.5 * (1.0 + gate)) * acc_ref[...]


def _ffn(x, mod, gain, wg, wu, wd, mod_base):
    bsz, seq, _ = x.shape
    tile = pl.BlockSpec((1, ROW_TILE, D_MODEL), lambda b, s: (b, s, 0))
    return pl.pallas_call(
        functools.partial(_ffn_kernel, mod_base=mod_base),
        grid=(bsz, seq // ROW_TILE),
        in_specs=[
            tile,
            pl.BlockSpec((1, N_MOD, D_MODEL), lambda b, s: (b, 0, 0)),
            _resident((1, D_MODEL)),
            _resident((N_FF_CHUNKS, D_MODEL, FF_CHUNK)),
            _resident((N_FF_CHUNKS, D_MODEL, FF_CHUNK)),
            _resident((N_FF_CHUNKS, FF_CHUNK, D_MODEL)),
        ],
        out_specs=tile,
        out_shape=jax.ShapeDtypeStruct(x.shape, F32),
        scratch_shapes=[pltpu.VMEM((ROW_TILE, D_MODEL), BF16),
                        pltpu.VMEM((ROW_TILE, D_MODEL), F32)],
        compiler_params=_params(2),
        name="ffn",
    )(x, mod, gain, wg, wu, wd)


def _head_norm(t, gain, mult):
    lane = lax.broadcasted_iota(jnp.int32, (1, LANES), 1)
    first = lane < HEAD_DIM
    outs = []
    for j in range(SB_WIDTH // LANES):
        sl = t[:, j * LANES:(j + 1) * LANES]
        sq = sl * sl
        ms_a = jnp.sum(jnp.where(first, sq, 0.0), axis=-1, keepdims=True) * (1.0 / HEAD_DIM)
        ms_b = jnp.sum(jnp.where(first, 0.0, sq), axis=-1, keepdims=True) * (1.0 / HEAD_DIM)
        inv = jnp.where(first, lax.rsqrt(ms_a + EPS), lax.rsqrt(ms_b + EPS))
        y = sl * inv * gain
        if mult != 1.0:
            y = y * mult
        outs.append(y)
    return jnp.concatenate(outs, axis=-1)


def _inproj_kernel(x_ref, mod_ref, gain_ref, win_ref, qg_ref, kg_ref, cw_ref, pw_ref, ps_ref,
                   q_ref, k_ref, v_ref, y_ref, h_ref, hist_ref, lvl_ref):
    s = pl.program_id(1)
    rows = ROW_TILE
    end = HIST_OFF + rows
    x = x_ref[0]
    m = mod_ref[0]
    h_ref[...] = _norm_mod(x, gain_ref[...], m[3:4], m[4:5]).astype(BF16)
    h = h_ref[...]

    q = jnp.dot(h, win_ref[:, 0:SB_WIDTH], preferred_element_type=F32)
    q_ref[0] = _head_norm(q, qg_ref[...], HEAD_DIM ** -0.5 * LOG2E).astype(BF16)
    k = jnp.dot(h, win_ref[:, SB_WIDTH:2 * SB_WIDTH], preferred_element_type=F32)
    k_ref[0] = _head_norm(k, kg_ref[...], 1.0).astype(BF16)
    v_ref[0] = jnp.dot(h, win_ref[:, 2 * SB_WIDTH:3 * SB_WIDTH], preferred_element_type=F32).astype(BF16)

    base = 3 * SB_WIDTH
    gate_b = jnp.dot(h, win_ref[:, base:base + CONV_WIDTH], preferred_element_type=F32)
    gate_c = jnp.dot(h, win_ref[:, base + CONV_WIDTH:base + 2 * CONV_WIDTH], preferred_element_type=F32)
    u = jnp.dot(h, win_ref[:, base + 2 * CONV_WIDTH:base + 3 * CONV_WIDTH], preferred_element_type=F32)
    p = jnp.dot(h, win_ref[:, base + 3 * CONV_WIDTH:IN_WIDTH], preferred_element_type=F32)
    cu = gate_c * u

    @pl.when(s == 0)
    def _():
        hist_ref[0:HIST_OFF, :] = jnp.zeros((HIST_OFF, CONV_WIDTH + POOL_WIDTH), F32)
        lvl_ref[:, 0:HIST_PAD, :] = jnp.zeros((3, HIST_PAD, POOL_WIDTH), F32)

    @pl.when(s > 0)
    def _():
        hist_ref[HIST_PAD:HIST_OFF, :] = hist_ref[rows + HIST_PAD:end, :]

    hist_ref[HIST_OFF:end, 0:CONV_WIDTH] = cu
    hist_ref[HIST_OFF:end, CONV_WIDTH:] = p

    cw = cw_ref[...]
    conv = (cw[0:1] * hist_ref[HIST_OFF - 2:end - 2, 0:CONV_WIDTH]
            + cw[1:2] * hist_ref[HIST_OFF - 1:end - 1, 0:CONV_WIDTH]
            + cw[2:3] * cu)
    y_conv = gate_b * conv

    lo = HIST_PAD
    lvl_ref[0, lo:end, :] = hist_ref[lo:end, CONV_WIDTH:] + hist_ref[lo - 1:end - 1, CONV_WIDTH:]
    lvl_ref[1, lo:end, :] = lvl_ref[0, lo:end, :] + lvl_ref[0, lo - 2:end - 2, :]
    lvl_ref[2, lo:end, :] = lvl_ref[1, lo:end, :] + lvl_ref[1, lo - 4:end - 4, :]
    s2 = lvl_ref[0, HIST_OFF:end, :]
    s4 = lvl_ref[1, HIST_OFF:end, :]
    s8 = lvl_ref[2, HIST_OFF:end, :]
    s16 = s8 + lvl_ref[2, HIST_OFF - 8:end - 8, :]

    t1 = (s * rows + 1 + lax.broadcasted_iota(jnp.int32, (rows, 1), 0)).astype(F32)
    inv = [1.0 / jnp.minimum(t1, float(w)) for w in POOL_WINDOWS]
    lane = lax.broadcasted_iota(jnp.int32, (1, LANES), 1)
    first = lane < POOL_WIDTH // len(POOL_WINDOWS)
    pooled = jnp.concatenate([
        jnp.where(first, s2[:, :LANES] * inv[0], s4[:, :LANES] * inv[1]),
        jnp.where(first, s8[:, LANES:] * inv[2], s16[:, LANES:] * inv[3]),
    ], axis=-1) - p
    y_pool = jnp.dot(pooled.astype(BF16), pw_ref[...], preferred_element_type=F32) * ps_ref[...]

    y_ref[0] = jnp.concatenate([y_conv, y_pool], axis=-1).astype(BF16)


def _inproj(x, mod, gain, w_in, q_gain, k_gain, conv_w, pool_bd, pool_scale):
    bsz, seq, _ = x.shape
    tile_in = pl.BlockSpec((1, ROW_TILE, D_MODEL), lambda b, s: (b, s, 0))
    tile_out = pl.BlockSpec((1, ROW_TILE, SB_WIDTH), lambda b, s: (b, s, 0))
    out = jax.ShapeDtypeStruct((bsz, seq, SB_WIDTH), BF16)
    return pl.pallas_call(
        _inproj_kernel,
        grid=(bsz, seq // ROW_TILE),
        in_specs=[
            tile_in,
            pl.BlockSpec((1, N_MOD, D_MODEL), lambda b, s: (b, 0, 0)),
            _resident((1, D_MODEL)),
            _resident((D_MODEL, IN_WIDTH)),
            _resident((1, LANES)),
            _resident((1, LANES)),
            _resident((CONV_K, CONV_WIDTH)),
            _resident((POOL_WIDTH, POOL_WIDTH)),
            _resident((1, POOL_WIDTH)),
        ],
        out_specs=[tile_out, tile_out, tile_out, tile_out],
        out_shape=[out, out, out, out],
        scratch_shapes=[
            pltpu.VMEM((ROW_TILE, D_MODEL), BF16),
            pltpu.VMEM((HIST_OFF + ROW_TILE, CONV_WIDTH + POOL_WIDTH), F32),
            pltpu.VMEM((3, HIST_OFF + ROW_TILE, POOL_WIDTH), F32),
        ],
        compiler_params=_params(2),
        name="inproj",
    )(x, mod, gain, w_in, q_gain, k_gain, conv_w, pool_bd, pool_scale)


def _attn_kernel(q_ref, k_ref, v_ref, tri_ref, o_ref, acc_ref, run_ref):
    i = pl.program_id(2)
    t = ATT_TILE
    q2 = q_ref[0]
    lane = lax.broadcasted_iota(jnp.int32, (1, LANES), 1)
    first = lane < HEAD_DIM
    zero = jnp.zeros_like(q2)
    q_heads = (jnp.where(first, q2, zero), jnp.where(first, zero, q2))
    tri = tri_ref[...]

    acc_ref[...] = jnp.zeros_like(acc_ref)
    run_ref[...] = jnp.zeros_like(run_ref)

    def sweep(j, past):
        start = pl.multiple_of(j * t, t)
        kb = k_ref[0, pl.ds(start, t), :]
        vb = v_ref[0, pl.ds(start, t), :]
        for hd in range(2):
            w = lax.dot_general(q_heads[hd], kb, (((1,), (1,)), ((), ())),
                                preferred_element_type=F32)
            w = jnp.minimum(w, Z_CLAMP)
            sp = jnp.log(1.0 + jnp.exp2(w)) * LOG2E
            if past is not None:
                sp = jnp.where(past, sp, 0.0)
            hi = sp.astype(BF16)
            lo = (sp - hi.astype(F32)).astype(BF16)
            cum = (jnp.dot(hi, tri, preferred_element_type=F32)
                   + jnp.dot(lo, tri, preferred_element_type=F32))
            run = run_ref[hd]
            a = jnp.exp2(((w - sp) - cum) - jnp.concatenate([run, run], axis=-1))
            if past is not None:
                a = jnp.where(past, a, 0.0)
            acc_ref[hd] += jnp.dot(a.astype(BF16), vb, preferred_element_type=F32)
            run_ref[hd] = run + jnp.sum(sp, axis=-1, keepdims=True)

    row = lax.broadcasted_iota(jnp.int32, (t, t), 0)
    col = lax.broadcasted_iota(jnp.int32, (t, t), 1)
    sweep(i, col < row)

    def body(jj, carry):
        sweep(i - 1 - jj, None)
        return carry

    lax.fori_loop(0, i, body, 0)
    o_ref[0] = jnp.where(first, acc_ref[0], acc_ref[1]).astype(BF16)


def _attention(q, k, v, tri):
    bsz, seq, _ = q.shape
    n_pairs = SB_WIDTH // LANES
    q_tile = pl.BlockSpec((1, ATT_TILE, LANES), lambda b, hp, i: (b, i, hp))
    kv_full = pl.BlockSpec((1, seq, LANES), lambda b, hp, i: (b, 0, hp))
    return pl.pallas_call(
        _attn_kernel,
        grid=(bsz, n_pairs, seq // ATT_TILE),
        in_specs=[q_tile, kv_full, kv_full, _resident((ATT_TILE, ATT_TILE))],
        out_specs=q_tile,
        out_shape=jax.ShapeDtypeStruct(q.shape, BF16),
        scratch_shapes=[pltpu.VMEM((2, ATT_TILE, LANES), F32),
                        pltpu.VMEM((2, ATT_TILE, LANES), F32)],
        compiler_params=_params(3),
        name="sb_attention",
    )(q, k, v, tri)


def _outproj_kernel(x_ref, mod_ref, ysb_ref, ycp_ref, wo_ref, o_ref):
    gate = mod_ref[0][5:6]
    mixed = (jnp.dot(ysb_ref[0], wo_ref[0:SB_WIDTH, :], preferred_element_type=F32)
             + jnp.dot(ycp_ref[0], wo_ref[SB_WIDTH:, :], preferred_element_type=F32))
    o_ref[0] = x_ref[0] + (1.0 + gate) * mixed


def _outproj(x, mod, y_sb, y_cp, w_out):
    bsz, seq, _ = x.shape
    tile = pl.BlockSpec((1, ROW_TILE, D_MODEL), lambda b, s: (b, s, 0))
    half = pl.BlockSpec((1, ROW_TILE, SB_WIDTH), lambda b, s: (b, s, 0))
    return pl.pallas_call(
        _outproj_kernel,
        grid=(bsz, seq // ROW_TILE),
        in_specs=[tile, pl.BlockSpec((1, N_MOD, D_MODEL), lambda b, s: (b, 0, 0)), half, half,
                  _resident((D_MODEL, D_MODEL))],
        out_specs=tile,
        out_shape=jax.ShapeDtypeStruct(x.shape, F32),
        compiler_params=_params(2),
        name="outproj",
    )(x, mod, y_sb, y_cp, w_out)


def _ffn_weights(w_gate, w_up, w_down):
    wg = w_gate.astype(BF16).reshape(D_MODEL, N_FF_CHUNKS, FF_CHUNK).transpose(1, 0, 2)
    wu = w_up.astype(BF16).reshape(D_MODEL, N_FF_CHUNKS, FF_CHUNK).transpose(1, 0, 2)
    wd = w_down.astype(BF16).reshape(N_FF_CHUNKS, FF_CHUNK, D_MODEL)
    return wg, wu, wd


def _pool_block_diag(pool_w):
    g, c, _ = pool_w.shape
    eye = jnp.eye(g, dtype=pool_w.dtype)
    return (eye[:, None, :, None] * pool_w[:, :, None, :]).reshape(g * c, g * c).astype(BF16)


def kernel(x, c, w_ada, b_ada, ffn1_norm, ffn1_gate, ffn1_up, ffn1_down, mix_norm, w_in, q_norm,
           k_norm, conv_w, pool_w, pool_scale, w_out, ffn2_norm, ffn2_gate, ffn2_up, ffn2_down):
    n_layers = w_ada.shape[0]
    bsz = x.shape[0]
    mod_all = _ada_all_layers(c, w_ada, b_ada).reshape(n_layers, bsz, N_MOD, D_MODEL)
    tri = jnp.tril(jnp.ones((ATT_TILE, ATT_TILE), BF16), -1)

    for l in range(n_layers):
        mod = mod_all[l]
        x = _ffn(x, mod, ffn1_norm[l][None], *_ffn_weights(ffn1_gate[l], ffn1_up[l], ffn1_down[l]),
                 mod_base=0)
        q, k, v, y_cp = _inproj(
            x, mod, mix_norm[l][None], w_in[l].astype(BF16),
            jnp.tile(q_norm[l], 2)[None], jnp.tile(k_norm[l], 2)[None],
            conv_w[l], _pool_block_diag(pool_w[l]), pool_scale[l][None])
        y_sb = _attention(q, k, v, tri)
        x = _outproj(x, mod, y_sb, y_cp, w_out[l].astype(BF16))
        x = _ffn(x, mod, ffn2_norm[l][None], *_ffn_weights(ffn2_gate[l], ffn2_up[l], ffn2_down[l]),
                 mod_base=6)
    return x
```

```python
import functools
import math

import jax
import jax.numpy as jnp
from jax import lax
from jax.experimental import pallas as pl
from jax.experimental.pallas import tpu as pltpu

F32 = jnp.float32
BF16 = jnp.bfloat16

D_MODEL = 1024
HEAD_DIM = 64
SB_WIDTH = 512
CONV_WIDTH = 256
CONV_K = 3
POOL_WINDOWS = (2, 4, 8, 16)
POOL_WIDTH = 256
IN_WIDTH = 3 * SB_WIDTH + 3 * CONV_WIDTH + POOL_WIDTH
D_FF = 2816
N_MOD = 9
EPS = 1e-6

LANES = 128
SUBLANES = 8
VMEM_LIMIT_BYTES = 56 * 1024 * 1024

ADA_TN = 1536
ROW_TILE = 512
FF_CHUNK = 256
N_FF_CHUNKS = D_FF // FF_CHUNK
ATT_TILE = 256
KEY_TILE = 512
Q_GROUP = 4
POOL_HALO = max(POOL_WINDOWS)
HIST_PAD = SUBLANES
HIST_OFF = HIST_PAD + POOL_HALO

LOG2E = math.log2(math.e)
LN2 = math.log(2.0)
Z_CLAMP = 126.0
MASKED_LOGIT = -1e30
RUN_DONE = 150.0


def _params(n_grid):
    return pltpu.CompilerParams(
        dimension_semantics=("arbitrary",) * n_grid,
        vmem_limit_bytes=VMEM_LIMIT_BYTES)


def _resident(shape):
    zeros = (0,) * len(shape)
    return pl.BlockSpec(shape, lambda *_: zeros, pipeline_mode=pl.Buffered(1))


def _ada_kernel(c_ref, w_ref, b_ref, o_ref):
    c = c_ref[...]
    cond = (c * jax.nn.sigmoid(c)).astype(BF16)
    w = w_ref[0].astype(BF16)
    o_ref[0] = jnp.dot(cond, w, preferred_element_type=F32) + b_ref[0]


def _ada_all_layers(c, w_ada, b_ada):
    n_layers, _, n_out = w_ada.shape
    bsz = c.shape[0]
    return pl.pallas_call(
        _ada_kernel,
        grid=(n_layers, n_out // ADA_TN),
        in_specs=[
            pl.BlockSpec((bsz, D_MODEL), lambda l, j: (0, 0)),
            pl.BlockSpec((1, D_MODEL, ADA_TN), lambda l, j: (l, 0, j)),
            pl.BlockSpec((1, 1, ADA_TN), lambda l, j: (l, 0, j)),
        ],
        out_specs=pl.BlockSpec((1, bsz, ADA_TN), lambda l, j: (l, 0, j)),
        out_shape=jax.ShapeDtypeStruct((n_layers, bsz, n_out), F32),
        compiler_params=_params(2),
        name="adaln",
    )(c, w_ada, b_ada.reshape(n_layers, 1, n_out))


def _norm_mod(x, gain, shift, scale):
    ms = jnp.mean(x * x, axis=-1, keepdims=True)
    y = x * lax.rsqrt(ms + EPS) * gain
    return y * (1.0 + scale) + shift


def _ffn_kernel(x_ref, mod_ref, gain_ref, wg_ref, wu_ref, wd_ref, o_ref, *, mod_base):
    x = x_ref[0]
    m = mod_ref[0]
    shift = m[mod_base:mod_base + 1]
    scale = m[mod_base + 1:mod_base + 2]
    gate = m[mod_base + 2:mod_base + 3]
    h = _norm_mod(x, gain_ref[...], shift, scale).astype(BF16)
    acc = None
    for c in range(N_FF_CHUNKS):
        g = jnp.dot(h, wg_ref[c], preferred_element_type=F32)
        u = jnp.dot(h, wu_ref[c], preferred_element_type=F32)
        a = (g * jax.nn.sigmoid(g) * u).astype(BF16)
        d = jnp.dot(a, wd_ref[c], preferred_element_type=F32)
        acc = d if acc is None else acc + d
    o_ref[0] = x + (0.5 * (1.0 + gate)) * acc


def _ffn(x, mod, gain, wg, wu, wd, mod_base):
    bsz, seq, _ = x.shape
    tile = pl.BlockSpec((1, ROW_TILE, D_MODEL), lambda b, s: (b, s, 0))
    return pl.pallas_call(
        functools.partial(_ffn_kernel, mod_base=mod_base),
        grid=(bsz, seq // ROW_TILE),
        in_specs=[
            tile,
            pl.BlockSpec((1, N_MOD, D_MODEL), lambda b, s: (b, 0, 0)),
            _resident((1, D_MODEL)),
            _resident((N_FF_CHUNKS, D_MODEL, FF_CHUNK)),
            _resident((N_FF_CHUNKS, D_MODEL, FF_CHUNK)),
            _resident((N_FF_CHUNKS, FF_CHUNK, D_MODEL)),
        ],
        out_specs=tile,
        out_shape=jax.ShapeDtypeStruct(x.shape, F32),
        compiler_params=_params(2),
        name="ffn",
    )(x, mod, gain, wg, wu, wd)


def _head_norm(t, gain, mult):
    lane = lax.broadcasted_iota(jnp.int32, (1, LANES), 1)
    first = lane < HEAD_DIM
    outs = []
    for j in range(SB_WIDTH // LANES):
        sl = t[:, j * LANES:(j + 1) * LANES]
        sq = sl * sl
        ms_a = jnp.sum(jnp.where(first, sq, 0.0), axis=-1, keepdims=True) * (1.0 / HEAD_DIM)
        ms_b = jnp.sum(jnp.where(first, 0.0, sq), axis=-1, keepdims=True) * (1.0 / HEAD_DIM)
        inv = jnp.where(first, lax.rsqrt(ms_a + EPS), lax.rsqrt(ms_b + EPS))
        y = sl * inv * gain
        if mult != 1.0:
            y = y * mult
        outs.append(y)
    return jnp.concatenate(outs, axis=-1)


def _inproj_kernel(x_ref, mod_ref, gain_ref, win_ref, qg_ref, kg_ref, cw_ref, pw_ref, ps_ref,
                   q_ref, k_ref, v_ref, y_ref, h_ref, hist_ref, lvl_ref):
    s = pl.program_id(1)
    rows = ROW_TILE
    end = HIST_OFF + rows
    x = x_ref[0]
    m = mod_ref[0]
    h_ref[...] = _norm_mod(x, gain_ref[...], m[3:4], m[4:5]).astype(BF16)
    h = h_ref[...]

    q = jnp.dot(h, win_ref[:, 0:SB_WIDTH], preferred_element_type=F32)
    q_ref[0] = _head_norm(q, qg_ref[...], HEAD_DIM ** -0.5 * LOG2E).astype(BF16)
    k = jnp.dot(h, win_ref[:, SB_WIDTH:2 * SB_WIDTH], preferred_element_type=F32)
    k_ref[0] = _head_norm(k, kg_ref[...], 1.0).astype(BF16)
    v_ref[0] = jnp.dot(h, win_ref[:, 2 * SB_WIDTH:3 * SB_WIDTH], preferred_element_type=F32).astype(BF16)

    base = 3 * SB_WIDTH
    gate_b = jnp.dot(h, win_ref[:, base:base + CONV_WIDTH], preferred_element_type=F32)
    gate_c = jnp.dot(h, win_ref[:, base + CONV_WIDTH:base + 2 * CONV_WIDTH], preferred_element_type=F32)
    u = jnp.dot(h, win_ref[:, base + 2 * CONV_WIDTH:base + 3 * CONV_WIDTH], preferred_element_type=F32)
    p = jnp.dot(h, win_ref[:, base + 3 * CONV_WIDTH:IN_WIDTH], preferred_element_type=F32)
    cu = gate_c * u

    @pl.when(s == 0)
    def _():
        hist_ref[0:HIST_OFF, :] = jnp.zeros((HIST_OFF, CONV_WIDTH + POOL_WIDTH), F32)
        lvl_ref[:, 0:HIST_PAD, :] = jnp.zeros((3, HIST_PAD, POOL_WIDTH), F32)

    @pl.when(s > 0)
    def _():
        hist_ref[HIST_PAD:HIST_OFF, :] = hist_ref[rows + HIST_PAD:end, :]

    hist_ref[HIST_OFF:end, 0:CONV_WIDTH] = cu
    hist_ref[HIST_OFF:end, CONV_WIDTH:] = p

    cw = cw_ref[...]
    conv = (cw[0:1] * hist_ref[HIST_OFF - 2:end - 2, 0:CONV_WIDTH]
            + cw[1:2] * hist_ref[HIST_OFF - 1:end - 1, 0:CONV_WIDTH]
            + cw[2:3] * cu)
    y_conv = gate_b * conv

    lo = HIST_PAD
    lvl_ref[0, lo:end, :] = hist_ref[lo:end, CONV_WIDTH:] + hist_ref[lo - 1:end - 1, CONV_WIDTH:]
    lvl_ref[1, lo:end, :] = lvl_ref[0, lo:end, :] + lvl_ref[0, lo - 2:end - 2, :]
    lvl_ref[2, lo:end, :] = lvl_ref[1, lo:end, :] + lvl_ref[1, lo - 4:end - 4, :]
    s2 = lvl_ref[0, HIST_OFF:end, :]
    s4 = lvl_ref[1, HIST_OFF:end, :]
    s8 = lvl_ref[2, HIST_OFF:end, :]
    s16 = s8 + lvl_ref[2, HIST_OFF - 8:end - 8, :]

    t1 = (s * rows + 1 + lax.broadcasted_iota(jnp.int32, (rows, 1), 0)).astype(F32)
    inv = [1.0 / jnp.minimum(t1, float(w)) for w in POOL_WINDOWS]
    lane = lax.broadcasted_iota(jnp.int32, (1, LANES), 1)
    first = lane < POOL_WIDTH // len(POOL_WINDOWS)
    pooled = jnp.concatenate([
        jnp.where(first, s2[:, :LANES] * inv[0], s4[:, :LANES] * inv[1]),
        jnp.where(first, s8[:, LANES:] * inv[2], s16[:, LANES:] * inv[3]),
    ], axis=-1) - p
    y_pool = jnp.dot(pooled.astype(BF16), pw_ref[...], preferred_element_type=F32) * ps_ref[...]

    y_ref[0] = jnp.concatenate([y_conv, y_pool], axis=-1).astype(BF16)


def _inproj(x, mod, gain, w_in, q_gain, k_gain, conv_w, pool_bd, pool_scale):
    bsz, seq, _ = x.shape
    tile_in = pl.BlockSpec((1, ROW_TILE, D_MODEL), lambda b, s: (b, s, 0))
    tile_out = pl.BlockSpec((1, ROW_TILE, SB_WIDTH), lambda b, s: (b, s, 0))
    out = jax.ShapeDtypeStruct((bsz, seq, SB_WIDTH), BF16)
    return pl.pallas_call(
        _inproj_kernel,
        grid=(bsz, seq // ROW_TILE),
        in_specs=[
            tile_in,
            pl.BlockSpec((1, N_MOD, D_MODEL), lambda b, s: (b, 0, 0)),
            _resident((1, D_MODEL)),
            _resident((D_MODEL, IN_WIDTH)),
            _resident((1, LANES)),
            _resident((1, LANES)),
            _resident((CONV_K, CONV_WIDTH)),
            _resident((POOL_WIDTH, POOL_WIDTH)),
            _resident((1, POOL_WIDTH)),
        ],
        out_specs=[tile_out, tile_out, tile_out, tile_out],
        out_shape=[out, out, out, out],
        scratch_shapes=[
            pltpu.VMEM((ROW_TILE, D_MODEL), BF16),
            pltpu.VMEM((HIST_OFF + ROW_TILE, CONV_WIDTH + POOL_WIDTH), F32),
            pltpu.VMEM((3, HIST_OFF + ROW_TILE, POOL_WIDTH), F32),
        ],
        compiler_params=_params(2),
        name="inproj",
    )(x, mod, gain, w_in, q_gain, k_gain, conv_w, pool_bd, pool_scale)


def _attn_kernel(q_ref, k_ref, v_ref, tri_ref, o_ref, acc_ref, run_ref):
    j = pl.program_id(2)
    tq, tk, tc = ATT_TILE, KEY_TILE, ATT_TILE
    n = 2 * tq
    lane = lax.broadcasted_iota(jnp.int32, (1, LANES), 1)
    first = lane < HEAD_DIM
    tri = tri_ref[...]

    def sweep(g, hi, first_sweep):
        i = j * Q_GROUP + g
        q2 = q_ref[0, pl.ds(pl.multiple_of(g * tq, tq), tq), :]
        zero = jnp.zeros_like(q2)
        qs = jnp.concatenate([jnp.where(first, q2, zero), jnp.where(first, zero, q2)], axis=0)
        lo = pl.multiple_of(jnp.maximum(hi - tk, 0), tc)
        kb = k_ref[0, pl.ds(lo, tk), :]
        vb = v_ref[0, pl.ds(lo, tk), :]
        w = lax.dot_general(qs, kb, (((1,), (1,)), ((), ())), preferred_element_type=F32)
        col = lax.broadcasted_iota(jnp.int32, (n, tk), 1)
        if first_sweep:
            row = lax.broadcasted_iota(jnp.int32, (n, tk), 0) & (tq - 1)
            visible = col - row < i * tq - lo
        else:
            visible = col < hi - lo
        w = jnp.where(visible, jnp.minimum(w, Z_CLAMP), MASKED_LOGIT)
        sp = jnp.log(1.0 + jnp.exp2(w)) * LOG2E
        hi_part = sp.astype(BF16)
        lo_part = (sp - hi_part.astype(F32)).astype(BF16)
        split = jnp.concatenate([hi_part[:, :tc], hi_part[:, tc:], lo_part[:, :tc], lo_part[:, tc:]], axis=0)
        c = jnp.dot(split, tri, preferred_element_type=F32)
        cum = jnp.concatenate([c[0:n] + c[2 * n:3 * n], c[n:2 * n] + c[3 * n:4 * n]], axis=1)
        rs0 = jnp.sum(sp[:, :tc], axis=-1, keepdims=True)
        rs1 = jnp.sum(sp[:, tc:], axis=-1, keepdims=True)
        if first_sweep:
            off0 = jnp.broadcast_to(rs1, (n, LANES))
            off = jnp.concatenate([off0, off0, jnp.zeros((n, 2 * LANES), F32)], axis=1)
        else:
            run = run_ref[g]
            off0 = run + rs1
            off = jnp.concatenate([off0, off0, run, run], axis=1)
        a = jnp.exp2(((w - sp) - cum) - off)
        pv = jnp.dot(a.astype(BF16), vb, preferred_element_type=F32)
        if first_sweep:
            acc_ref[g] = pv
        else:
            acc_ref[g] += pv
        run_ref[g] = off0 + rs0

    for g in range(Q_GROUP):
        sweep(g, (j * Q_GROUP + g + 1) * tq, True)

    def finish_tile(g, carry):
        def more(state):
            hi, min_run = state
            return jnp.logical_and(hi > 0, min_run < RUN_DONE)

        def step(state):
            hi, _ = state
            sweep(g, hi, False)
            return jnp.maximum(hi - tk, 0), jnp.min(run_ref[g])

        hi1 = jnp.maximum((j * Q_GROUP + g + 1) * tq - tk, 0)
        lax.while_loop(more, step, (hi1, jnp.min(run_ref[g])))
        return carry

    lax.fori_loop(0, Q_GROUP, finish_tile, 0)
    for g in range(Q_GROUP):
        acc = acc_ref[g]
        o_ref[0, g * tq:(g + 1) * tq, :] = jnp.where(first, acc[0:tq], acc[tq:n]).astype(BF16)


def _attention(q, k, v, tri):
    bsz, seq, _ = q.shape
    n_pairs = SB_WIDTH // LANES
    rows = Q_GROUP * ATT_TILE
    q_tile = pl.BlockSpec((1, rows, LANES), lambda b, hp, j: (b, j, hp))
    kv_full = pl.BlockSpec((1, seq, LANES), lambda b, hp, j: (b, 0, hp))
    return pl.pallas_call(
        _attn_kernel,
        grid=(bsz, n_pairs, seq // rows),
        in_specs=[q_tile, kv_full, kv_full, _resident((ATT_TILE, ATT_TILE))],
        out_specs=q_tile,
        out_shape=jax.ShapeDtypeStruct(q.shape, BF16),
        scratch_shapes=[pltpu.VMEM((Q_GROUP, 2 * ATT_TILE, LANES), F32),
                        pltpu.VMEM((Q_GROUP, 2 * ATT_TILE, LANES), F32)],
        compiler_params=_params(3),
        name="sb_attention",
    )(q, k, v, tri)


def _outproj_kernel(x_ref, mod_ref, ysb_ref, ycp_ref, wo_ref, o_ref):
    gate = mod_ref[0][5:6]
    mixed = (jnp.dot(ysb_ref[0], wo_ref[0:SB_WIDTH, :], preferred_element_type=F32)
             + jnp.dot(ycp_ref[0], wo_ref[SB_WIDTH:, :], preferred_element_type=F32))
    o_ref[0] = x_ref[0] + (1.0 + gate) * mixed


def _outproj(x, mod, y_sb, y_cp, w_out):
    bsz, seq, _ = x.shape
    tile = pl.BlockSpec((1, ROW_TILE, D_MODEL), lambda b, s: (b, s, 0))
    half = pl.BlockSpec((1, ROW_TILE, SB_WIDTH), lambda b, s: (b, s, 0))
    return pl.pallas_call(
        _outproj_kernel,
        grid=(bsz, seq // ROW_TILE),
        in_specs=[tile, pl.BlockSpec((1, N_MOD, D_MODEL), lambda b, s: (b, 0, 0)), half, half,
                  _resident((D_MODEL, D_MODEL))],
        out_specs=tile,
        out_shape=jax.ShapeDtypeStruct(x.shape, F32),
        compiler_params=_params(2),
        name="outproj",
    )(x, mod, y_sb, y_cp, w_out)


def _ffn_weights(w_gate, w_up, w_down):
    wg = w_gate.astype(BF16).reshape(D_MODEL, N_FF_CHUNKS, FF_CHUNK).transpose(1, 0, 2)
    wu = w_up.astype(BF16).reshape(D_MODEL, N_FF_CHUNKS, FF_CHUNK).transpose(1, 0, 2)
    wd = w_down.astype(BF16).reshape(N_FF_CHUNKS, FF_CHUNK, D_MODEL)
    return wg, wu, wd


def _pool_block_diag(pool_w):
    g, c, _ = pool_w.shape
    eye = jnp.eye(g, dtype=pool_w.dtype)
    return (eye[:, None, :, None] * pool_w[:, :, None, :]).reshape(g * c, g * c).astype(BF16)


def kernel(x, c, w_ada, b_ada, ffn1_norm, ffn1_gate, ffn1_up, ffn1_down, mix_norm, w_in, q_norm,
           k_norm, conv_w, pool_w, pool_scale, w_out, ffn2_norm, ffn2_gate, ffn2_up, ffn2_down):
    n_layers = w_ada.shape[0]
    bsz = x.shape[0]
    mod_all = _ada_all_layers(c, w_ada, b_ada).reshape(n_layers, bsz, N_MOD, D_MODEL)
    tri = jnp.tril(jnp.ones((ATT_TILE, ATT_TILE), BF16), -1)

    for l in range(n_layers):
        mod = mod_all[l]
        x = _ffn(x, mod, ffn1_norm[l][None], *_ffn_weights(ffn1_gate[l], ffn1_up[l], ffn1_down[l]),
                 mod_base=0)
        q, k, v, y_cp = _inproj(
            x, mod, mix_norm[l][None], w_in[l].astype(BF16),
            jnp.tile(q_norm[l], 2)[None], jnp.tile(k_norm[l], 2)[None],
            conv_w[l], _pool_block_diag(pool_w[l]), pool_scale[l][None])
        y_sb = _attention(q, k, v, tri)
        x = _outproj(x, mod, y_sb, y_cp, w_out[l].astype(BF16))
        x = _ffn(x, mod, ffn2_norm[l][None], *_ffn_weights(ffn2_gate[l], ffn2_up[l], ffn2_down[l]),
                 mod_base=6)
    return x
```

```python
import functools
import math

import jax
import jax.numpy as jnp
from jax import lax
from jax.experimental import pallas as pl
from jax.experimental.pallas import tpu as pltpu

F32 = jnp.float32
BF16 = jnp.bfloat16

D_MODEL = 1024
HEAD_DIM = 64
SB_WIDTH = 512
CONV_WIDTH = 256
CONV_K = 3
POOL_WINDOWS = (2, 4, 8, 16)
POOL_WIDTH = 256
IN_WIDTH = 3 * SB_WIDTH + 3 * CONV_WIDTH + POOL_WIDTH
D_FF = 2816
N_MOD = 9
MIX_GATE_ROW = 5
EPS = 1e-6

LANES = 128
SUBLANES = 8
VMEM_LIMIT_BYTES = 56 * 1024 * 1024

ADA_TN = 1536
ROW_TILE = 512
FF_CHUNK = 256
N_FF_CHUNKS = D_FF // FF_CHUNK
ATT_TILE = 256
KEY_TILE = 512
Q_GROUP = 8
CUM_PASSES = 1
POOL_HALO = max(POOL_WINDOWS)
HIST_PAD = SUBLANES
HIST_OFF = HIST_PAD + POOL_HALO

LOG2E = math.log2(math.e)
LN2 = math.log(2.0)
Z_CLAMP = 126.0
MASKED_LOGIT = -1e30
RUN_DONE = 150.0


def _params(n_grid):
    return pltpu.CompilerParams(
        dimension_semantics=("arbitrary",) * n_grid,
        vmem_limit_bytes=VMEM_LIMIT_BYTES)


def _resident(shape):
    zeros = (0,) * len(shape)
    return pl.BlockSpec(shape, lambda *_: zeros, pipeline_mode=pl.Buffered(1))


def _ada_kernel(c_ref, w_ref, b_ref, o_ref):
    c = c_ref[...]
    cond = (c * jax.nn.sigmoid(c)).astype(BF16)
    w = w_ref[0].astype(BF16)
    o_ref[0] = jnp.dot(cond, w, preferred_element_type=F32) + b_ref[0]


def _ada_all_layers(c, w_ada, b_ada):
    n_layers, _, n_out = w_ada.shape
    bsz = c.shape[0]
    return pl.pallas_call(
        _ada_kernel,
        grid=(n_layers, n_out // ADA_TN),
        in_specs=[
            pl.BlockSpec((bsz, D_MODEL), lambda l, j: (0, 0)),
            pl.BlockSpec((1, D_MODEL, ADA_TN), lambda l, j: (l, 0, j)),
            pl.BlockSpec((1, 1, ADA_TN), lambda l, j: (l, 0, j)),
        ],
        out_specs=pl.BlockSpec((1, bsz, ADA_TN), lambda l, j: (l, 0, j)),
        out_shape=jax.ShapeDtypeStruct((n_layers, bsz, n_out), F32),
        compiler_params=_params(2),
        name="adaln",
    )(c, w_ada, b_ada.reshape(n_layers, 1, n_out))


def _norm_mod(x, gain, shift, scale):
    ms = jnp.mean(x * x, axis=-1, keepdims=True)
    y = x * lax.rsqrt(ms + EPS) * gain
    return y * (1.0 + scale) + shift


def _ffn_kernel(*refs, mod_base, mix):
    if mix:
        x_ref, mod_ref, ysb_ref, ycp_ref, wo_ref, gain_ref, wg_ref, wu_ref, wd_ref, o_ref = refs
    else:
        x_ref, mod_ref, gain_ref, wg_ref, wu_ref, wd_ref, o_ref = refs
    x = x_ref[0]
    m = mod_ref[0]
    if mix:
        mixed = (jnp.dot(ysb_ref[0], wo_ref[0:SB_WIDTH, :], preferred_element_type=F32)
                 + jnp.dot(ycp_ref[0], wo_ref[SB_WIDTH:, :], preferred_element_type=F32))
        x = x + (1.0 + m[MIX_GATE_ROW:MIX_GATE_ROW + 1]) * mixed
    shift = m[mod_base:mod_base + 1]
    scale = m[mod_base + 1:mod_base + 2]
    gate = m[mod_base + 2:mod_base + 3]
    h = _norm_mod(x, gain_ref[...], shift, scale).astype(BF16)
    acc = None
    for c in range(N_FF_CHUNKS):
        cols = slice(c * FF_CHUNK, (c + 1) * FF_CHUNK)
        g = jnp.dot(h, wg_ref[:, cols], preferred_element_type=F32)
        u = jnp.dot(h, wu_ref[:, cols], preferred_element_type=F32)
        a = (g * jax.nn.sigmoid(g) * u).astype(BF16)
        d = jnp.dot(a, wd_ref[cols, :], preferred_element_type=F32)
        acc = d if acc is None else acc + d
    o_ref[0] = x + (0.5 * (1.0 + gate)) * acc


def _ffn(x, mod, gain, wg, wu, wd, mod_base, mix=None):
    bsz, seq, _ = x.shape
    tile = pl.BlockSpec((1, ROW_TILE, D_MODEL), lambda b, s: (b, s, 0))
    half = pl.BlockSpec((1, ROW_TILE, SB_WIDTH), lambda b, s: (b, s, 0))
    mod_spec = pl.BlockSpec((1, N_MOD, D_MODEL), lambda b, s: (b, 0, 0))
    weights = [_resident((1, D_MODEL)), _resident((D_MODEL, D_FF)), _resident((D_MODEL, D_FF)),
               _resident((D_FF, D_MODEL))]
    if mix is None:
        in_specs, args = [tile, mod_spec] + weights, (x, mod, gain, wg, wu, wd)
    else:
        in_specs = [tile, mod_spec, half, half, _resident((D_MODEL, D_MODEL))] + weights
        args = (x, mod) + tuple(mix) + (gain, wg, wu, wd)
    return pl.pallas_call(
        functools.partial(_ffn_kernel, mod_base=mod_base, mix=mix is not None),
        grid=(bsz, seq // ROW_TILE),
        in_specs=in_specs,
        out_specs=tile,
        out_shape=jax.ShapeDtypeStruct(x.shape, F32),
        compiler_params=_params(2),
        name="ffn_mix" if mix is not None else "ffn",
    )(*args)


def _head_norm(t, gain, mult):
    lane = lax.broadcasted_iota(jnp.int32, (1, LANES), 1)
    first = lane < HEAD_DIM
    outs = []
    for j in range(SB_WIDTH // LANES):
        sl = t[:, j * LANES:(j + 1) * LANES]
        sq = sl * sl
        ms_a = jnp.sum(jnp.where(first, sq, 0.0), axis=-1, keepdims=True) * (1.0 / HEAD_DIM)
        ms_b = jnp.sum(jnp.where(first, 0.0, sq), axis=-1, keepdims=True) * (1.0 / HEAD_DIM)
        inv = jnp.where(first, lax.rsqrt(ms_a + EPS), lax.rsqrt(ms_b + EPS))
        y = sl * inv * gain
        if mult != 1.0:
            y = y * mult
        outs.append(y)
    return jnp.concatenate(outs, axis=-1)


def _inproj_kernel(x_ref, mod_ref, gain_ref, win_ref, qg_ref, kg_ref, cw_ref, pw_ref, ps_ref,
                   q_ref, k_ref, v_ref, y_ref, h_ref, hist_ref, lvl_ref):
    s = pl.program_id(1)
    rows = ROW_TILE
    end = HIST_OFF + rows
    x = x_ref[0]
    m = mod_ref[0]
    h_ref[...] = _norm_mod(x, gain_ref[...], m[3:4], m[4:5]).astype(BF16)
    h = h_ref[...]

    q = jnp.dot(h, win_ref[:, 0:SB_WIDTH], preferred_element_type=F32)
    q_ref[0] = _head_norm(q, qg_ref[...], HEAD_DIM ** -0.5 * LOG2E).astype(BF16)
    k = jnp.dot(h, win_ref[:, SB_WIDTH:2 * SB_WIDTH], preferred_element_type=F32)
    k_ref[0] = _head_norm(k, kg_ref[...], 1.0).astype(BF16)
    v_ref[0] = jnp.dot(h, win_ref[:, 2 * SB_WIDTH:3 * SB_WIDTH], preferred_element_type=F32).astype(BF16)

    base = 3 * SB_WIDTH
    gate_b = jnp.dot(h, win_ref[:, base:base + CONV_WIDTH], preferred_element_type=F32)
    gate_c = jnp.dot(h, win_ref[:, base + CONV_WIDTH:base + 2 * CONV_WIDTH], preferred_element_type=F32)
    u = jnp.dot(h, win_ref[:, base + 2 * CONV_WIDTH:base + 3 * CONV_WIDTH], preferred_element_type=F32)
    p = jnp.dot(h, win_ref[:, base + 3 * CONV_WIDTH:IN_WIDTH], preferred_element_type=F32)
    cu = gate_c * u

    @pl.when(s == 0)
    def _():
        hist_ref[0:HIST_OFF, :] = jnp.zeros((HIST_OFF, CONV_WIDTH + POOL_WIDTH), F32)
        lvl_ref[:, 0:HIST_PAD, :] = jnp.zeros((3, HIST_PAD, POOL_WIDTH), F32)

    @pl.when(s > 0)
    def _():
        hist_ref[HIST_PAD:HIST_OFF, :] = hist_ref[rows + HIST_PAD:end, :]

    hist_ref[HIST_OFF:end, 0:CONV_WIDTH] = cu
    hist_ref[HIST_OFF:end, CONV_WIDTH:] = p

    cw = cw_ref[...]
    conv = (cw[0:1] * hist_ref[HIST_OFF - 2:end - 2, 0:CONV_WIDTH]
            + cw[1:2] * hist_ref[HIST_OFF - 1:end - 1, 0:CONV_WIDTH]
            + cw[2:3] * cu)
    y_conv = gate_b * conv

    lo = HIST_PAD
    lvl_ref[0, lo:end, :] = hist_ref[lo:end, CONV_WIDTH:] + hist_ref[lo - 1:end - 1, CONV_WIDTH:]
    lvl_ref[1, lo:end, :] = lvl_ref[0, lo:end, :] + lvl_ref[0, lo - 2:end - 2, :]
    lvl_ref[2, lo:end, :] = lvl_ref[1, lo:end, :] + lvl_ref[1, lo - 4:end - 4, :]
    s2 = lvl_ref[0, HIST_OFF:end, :]
    s4 = lvl_ref[1, HIST_OFF:end, :]
    s8 = lvl_ref[2, HIST_OFF:end, :]
    s16 = s8 + lvl_ref[2, HIST_OFF - 8:end - 8, :]

    t1 = (s * rows + 1 + lax.broadcasted_iota(jnp.int32, (rows, 1), 0)).astype(F32)
    inv = [1.0 / jnp.minimum(t1, float(w)) for w in POOL_WINDOWS]
    lane = lax.broadcasted_iota(jnp.int32, (1, LANES), 1)
    first = lane < POOL_WIDTH // len(POOL_WINDOWS)
    pooled = jnp.concatenate([
        jnp.where(first, s2[:, :LANES] * inv[0], s4[:, :LANES] * inv[1]),
        jnp.where(first, s8[:, LANES:] * inv[2], s16[:, LANES:] * inv[3]),
    ], axis=-1) - p
    y_pool = jnp.dot(pooled.astype(BF16), pw_ref[...], preferred_element_type=F32) * ps_ref[...]

    y_ref[0] = jnp.concatenate([y_conv, y_pool], axis=-1).astype(BF16)


def _inproj(x, mod, gain, w_in, q_gain, k_gain, conv_w, pool_bd, pool_scale):
    bsz, seq, _ = x.shape
    tile_in = pl.BlockSpec((1, ROW_TILE, D_MODEL), lambda b, s: (b, s, 0))
    tile_out = pl.BlockSpec((1, ROW_TILE, SB_WIDTH), lambda b, s: (b, s, 0))
    out = jax.ShapeDtypeStruct((bsz, seq, SB_WIDTH), BF16)
    return pl.pallas_call(
        _inproj_kernel,
        grid=(bsz, seq // ROW_TILE),
        in_specs=[
            tile_in,
            pl.BlockSpec((1, N_MOD, D_MODEL), lambda b, s: (b, 0, 0)),
            _resident((1, D_MODEL)),
            _resident((D_MODEL, IN_WIDTH)),
            _resident((1, LANES)),
            _resident((1, LANES)),
            _resident((CONV_K, CONV_WIDTH)),
            _resident((POOL_WIDTH, POOL_WIDTH)),
            _resident((1, POOL_WIDTH)),
        ],
        out_specs=[tile_out, tile_out, tile_out, tile_out],
        out_shape=[out, out, out, out],
        scratch_shapes=[
            pltpu.VMEM((ROW_TILE, D_MODEL), BF16),
            pltpu.VMEM((HIST_OFF + ROW_TILE, CONV_WIDTH + POOL_WIDTH), F32),
            pltpu.VMEM((3, HIST_OFF + ROW_TILE, POOL_WIDTH), F32),
        ],
        compiler_params=_params(2),
        name="inproj",
    )(x, mod, gain, w_in, q_gain, k_gain, conv_w, pool_bd, pool_scale)


def _attn_kernel(q_ref, k_ref, v_ref, tri_ref, bias_ref, o_ref, acc_ref, run_ref):
    j = pl.program_id(2)
    tq, tk, tc = ATT_TILE, KEY_TILE, ATT_TILE
    n = 2 * tq
    lane = lax.broadcasted_iota(jnp.int32, (1, LANES), 1)
    first = lane < HEAD_DIM
    tri = tri_ref[...]

    def logits(g, hi):
        q2 = q_ref[0, pl.ds(pl.multiple_of(g * tq, tq), tq), :]
        zero = jnp.zeros_like(q2)
        qs = jnp.concatenate([jnp.where(first, q2, zero), jnp.where(first, zero, q2)], axis=0)
        lo = pl.multiple_of(jnp.maximum(hi - tk, 0), tc)
        kb = k_ref[0, pl.ds(lo, tk), :]
        return lax.dot_general(qs, kb, (((1,), (1,)), ((), ())), preferred_element_type=F32), lo

    def softplus_parts(w):
        sp = jnp.log(1.0 + jnp.exp2(w)) * LOG2E
        sp_bf = sp.astype(BF16)
        parts = [sp_bf[:, :tc], sp_bf[:, tc:]]
        if CUM_PASSES == 2:
            rest = (sp - sp_bf.astype(F32)).astype(BF16)
            parts += [rest[:, :tc], rest[:, tc:]]
        rs0 = jnp.sum(sp[:, :tc], axis=-1, keepdims=True)
        rs1 = jnp.sum(sp[:, tc:], axis=-1, keepdims=True)
        return jnp.concatenate(parts, axis=0), rs0, rs1, w - sp

    def suffix_in_tile(split):
        c = jnp.dot(split, tri, preferred_element_type=F32)
        if CUM_PASSES == 2:
            return c[0:n] + c[2 * n:3 * n], c[n:2 * n] + c[3 * n:4 * n]
        return c[0:n], c[n:2 * n]

    def masked_logits(g):
        w, lo = logits(g, (j * Q_GROUP + g + 1) * tq)
        if g == 0:
            col = lax.broadcasted_iota(jnp.int32, (n, tk), 1)
            row = lax.broadcasted_iota(jnp.int32, (n, tk), 0) & (tq - 1)
            w = jnp.where(col - row < j * Q_GROUP * tq - lo, jnp.minimum(w, Z_CLAMP), MASKED_LOGIT)
        else:
            bias = jnp.concatenate([bias_ref[...], bias_ref[...]], axis=0)
            w = jnp.concatenate([jnp.minimum(w[:, :tc], Z_CLAMP),
                                 jnp.minimum(w[:, tc:] + bias, Z_CLAMP)], axis=1)
        return w, lo

    def first_weights(parts, cum):
        _, _, rs1, part = parts
        a0 = jnp.exp2((part[:, :tc] - cum[0]) - rs1)
        a1 = jnp.exp2(part[:, tc:] - cum[1])
        return jnp.concatenate([a0, a1], axis=1).astype(BF16)

    def first_output(g, lo, parts, a):
        vb = v_ref[0, pl.ds(lo, tk), :]
        acc_ref[g] = jnp.dot(a, vb, preferred_element_type=F32)
        run_ref[g] = jnp.broadcast_to(parts[1] + parts[2], (n, LANES))

    tiles = range(Q_GROUP)
    lows, parts, cums, weights = {}, {}, {}, {}
    for s in range(Q_GROUP + 2):
        if s < Q_GROUP:
            w, lows[s] = masked_logits(s)
        if 0 <= s - 1 < Q_GROUP:
            cums[s - 1] = suffix_in_tile(parts[s - 1][0])
        if 0 <= s - 2 < Q_GROUP:
            first_output(s - 2, lows[s - 2], parts[s - 2], weights[s - 2])
        if s < Q_GROUP:
            parts[s] = softplus_parts(w)
        if 0 <= s - 1 < Q_GROUP:
            weights[s - 1] = first_weights(parts[s - 1], cums[s - 1])

    def sweep(g, hi):
        w, lo = logits(g, hi)
        col = lax.broadcasted_iota(jnp.int32, (n, tk), 1)
        w = jnp.where(col < hi - lo, jnp.minimum(w, Z_CLAMP), MASKED_LOGIT)
        split, rs0, rs1, part = softplus_parts(w)
        cum0, cum1 = suffix_in_tile(split)
        run = run_ref[g]
        off0 = run + rs1
        a0 = jnp.exp2((part[:, :tc] - cum0) - jnp.concatenate([off0, off0], axis=1))
        a1 = jnp.exp2((part[:, tc:] - cum1) - jnp.concatenate([run, run], axis=1))
        a = jnp.concatenate([a0, a1], axis=1).astype(BF16)
        vb = v_ref[0, pl.ds(lo, tk), :]
        acc_ref[g] += jnp.dot(a, vb, preferred_element_type=F32)
        run_ref[g] = off0 + rs0

    def finish_tile(g, carry):
        def more(state):
            hi, min_run = state
            return jnp.logical_and(hi > 0, min_run < RUN_DONE)

        def step(state):
            hi, _ = state
            sweep(g, hi)
            return jnp.maximum(hi - tk, 0), jnp.min(run_ref[g])

        hi1 = jnp.maximum((j * Q_GROUP + g + 1) * tq - tk, 0)
        lax.while_loop(more, step, (hi1, jnp.min(run_ref[g])))
        return carry

    pending = None
    for g in tiles:
        unswept = (j * Q_GROUP + g + 1) * tq > tk
        run_g = jnp.where(unswept, run_ref[g], RUN_DONE)
        pending = run_g if pending is None else jnp.minimum(pending, run_g)

    @pl.when(jnp.min(pending) < RUN_DONE)
    def _():
        lax.fori_loop(0, Q_GROUP, finish_tile, 0)

    for g in tiles:
        acc = acc_ref[g]
        o_ref[0, g * tq:(g + 1) * tq, :] = jnp.where(first, acc[0:tq], acc[tq:n]).astype(BF16)


def _attention(q, k, v, tri, causal_bias):
    bsz, seq, _ = q.shape
    n_pairs = SB_WIDTH // LANES
    rows = Q_GROUP * ATT_TILE
    q_tile = pl.BlockSpec((1, rows, LANES), lambda b, hp, j: (b, j, hp))
    kv_full = pl.BlockSpec((1, seq, LANES), lambda b, hp, j: (b, 0, hp))
    return pl.pallas_call(
        _attn_kernel,
        grid=(bsz, n_pairs, seq // rows),
        in_specs=[q_tile, kv_full, kv_full, _resident((ATT_TILE, ATT_TILE)), _resident((ATT_TILE, ATT_TILE))],
        out_specs=q_tile,
        out_shape=jax.ShapeDtypeStruct(q.shape, BF16),
        scratch_shapes=[pltpu.VMEM((Q_GROUP, 2 * ATT_TILE, LANES), F32),
                        pltpu.VMEM((Q_GROUP, 2 * ATT_TILE, LANES), F32)],
        compiler_params=_params(3),
        name="sb_attention",
    )(q, k, v, tri, causal_bias)


def _ffn_weights(w_gate, w_up, w_down):
    return w_gate.astype(BF16), w_up.astype(BF16), w_down.astype(BF16)


def _pool_block_diag(pool_w):
    g, c, _ = pool_w.shape
    eye = jnp.eye(g, dtype=pool_w.dtype)
    return (eye[:, None, :, None] * pool_w[:, :, None, :]).reshape(g * c, g * c).astype(BF16)


def kernel(x, c, w_ada, b_ada, ffn1_norm, ffn1_gate, ffn1_up, ffn1_down, mix_norm, w_in, q_norm,
           k_norm, conv_w, pool_w, pool_scale, w_out, ffn2_norm, ffn2_gate, ffn2_up, ffn2_down):
    n_layers = w_ada.shape[0]
    bsz = x.shape[0]
    mod_all = _ada_all_layers(c, w_ada, b_ada).reshape(n_layers, bsz, N_MOD, D_MODEL)
    lower = jnp.tril(jnp.ones((ATT_TILE, ATT_TILE), F32), -1)
    tri = lower.astype(BF16)
    causal_bias = (1.0 - lower) * MASKED_LOGIT

    for l in range(n_layers):
        mod = mod_all[l]
        x = _ffn(x, mod, ffn1_norm[l][None], *_ffn_weights(ffn1_gate[l], ffn1_up[l], ffn1_down[l]),
                 mod_base=0)
        q, k, v, y_cp = _inproj(
            x, mod, mix_norm[l][None], w_in[l].astype(BF16),
            jnp.tile(q_norm[l], 2)[None], jnp.tile(k_norm[l], 2)[None],
            conv_w[l], _pool_block_diag(pool_w[l]), pool_scale[l][None])
        y_sb = _attention(q, k, v, tri, causal_bias)
        x = _ffn(x, mod, ffn2_norm[l][None], *_ffn_weights(ffn2_gate[l], ffn2_up[l], ffn2_down[l]),
                 mod_base=6, mix=(y_sb, y_cp, w_out[l].astype(BF16)))
    return x
```

```python
import functools
import math

import jax
import jax.numpy as jnp
from jax import lax
from jax.experimental import pallas as pl
from jax.experimental.pallas import tpu as pltpu

F32 = jnp.float32
BF16 = jnp.bfloat16

D_MODEL = 1024
HEAD_DIM = 64
SB_WIDTH = 512
CONV_WIDTH = 256
CONV_K = 3
POOL_WINDOWS = (2, 4, 8, 16)
POOL_WIDTH = 256
IN_WIDTH = 3 * SB_WIDTH + 3 * CONV_WIDTH + POOL_WIDTH
D_FF = 2816
N_MOD = 9
MIX_GATE_ROW = 5
EPS = 1e-6

LANES = 128
SUBLANES = 8
VMEM_LIMIT_BYTES = 56 * 1024 * 1024

ADA_TN = 1536
SUB_ROWS = 512
SUBS_PER_TILE = 2
ROW_TILE = SUB_ROWS * SUBS_PER_TILE
FF_CHUNK = 256
N_FF_CHUNKS = D_FF // FF_CHUNK
ATT_TILE = 256
KEY_TILE = 512
Q_GROUP = 8
CUM_PASSES = 1
POOL_HALO = max(POOL_WINDOWS)
HIST_PAD = SUBLANES
HIST_OFF = HIST_PAD + POOL_HALO

LOG2E = math.log2(math.e)
LN2 = math.log(2.0)
Z_CLAMP = 126.0
MASKED_LOGIT = -1e30
RUN_DONE = 150.0


def _params(n_grid):
    return pltpu.CompilerParams(
        dimension_semantics=("arbitrary",) * n_grid,
        vmem_limit_bytes=VMEM_LIMIT_BYTES)


def _resident(shape):
    zeros = (0,) * len(shape)
    return pl.BlockSpec(shape, lambda *_: zeros, pipeline_mode=pl.Buffered(1))


def _ada_kernel(c_ref, w_ref, b_ref, o_ref):
    c = c_ref[...]
    cond = (c * jax.nn.sigmoid(c)).astype(BF16)
    w = w_ref[0].astype(BF16)
    o_ref[0] = jnp.dot(cond, w, preferred_element_type=F32) + b_ref[0]


def _ada_all_layers(c, w_ada, b_ada):
    n_layers, _, n_out = w_ada.shape
    bsz = c.shape[0]
    return pl.pallas_call(
        _ada_kernel,
        grid=(n_layers, n_out // ADA_TN),
        in_specs=[
            pl.BlockSpec((bsz, D_MODEL), lambda l, j: (0, 0)),
            pl.BlockSpec((1, D_MODEL, ADA_TN), lambda l, j: (l, 0, j)),
            pl.BlockSpec((1, 1, ADA_TN), lambda l, j: (l, 0, j)),
        ],
        out_specs=pl.BlockSpec((1, bsz, ADA_TN), lambda l, j: (l, 0, j)),
        out_shape=jax.ShapeDtypeStruct((n_layers, bsz, n_out), F32),
        compiler_params=_params(2),
        name="adaln",
    )(c, w_ada, b_ada.reshape(n_layers, 1, n_out))


def _norm_mod(x, gain, shift, scale):
    ms = jnp.mean(x * x, axis=-1, keepdims=True)
    y = x * lax.rsqrt(ms + EPS) * gain
    return y * (1.0 + scale) + shift


def _ffn_kernel(*refs, mod_base, mix):
    if mix:
        x_ref, mod_ref, ysb_ref, ycp_ref, wo_ref, gain_ref, wg_ref, wu_ref, wd_ref, o_ref = refs
    else:
        x_ref, mod_ref, gain_ref, wg_ref, wu_ref, wd_ref, o_ref = refs
    m = mod_ref[0]
    xs = []
    for rows in _sub_tiles():
        x = x_ref[0, rows, :]
        if mix:
            mixed = (jnp.dot(ysb_ref[0, rows, :], wo_ref[0:SB_WIDTH, :], preferred_element_type=F32)
                     + jnp.dot(ycp_ref[0, rows, :], wo_ref[SB_WIDTH:, :], preferred_element_type=F32))
            x = x + (1.0 + m[MIX_GATE_ROW:MIX_GATE_ROW + 1]) * mixed
        xs.append(x)
    for rows, x in zip(_sub_tiles(), xs):
        o_ref[0, rows, :] = _swiglu_residual(x, m, mod_base, gain_ref, wg_ref, wu_ref, wd_ref)


def _sub_tiles():
    return [slice(t * SUB_ROWS, (t + 1) * SUB_ROWS) for t in range(SUBS_PER_TILE)]


def _swiglu_residual(x, m, mod_base, gain_ref, wg_ref, wu_ref, wd_ref, after_first_chunk=None):
    shift = m[mod_base:mod_base + 1]
    scale = m[mod_base + 1:mod_base + 2]
    gate = m[mod_base + 2:mod_base + 3]
    h = _norm_mod(x, gain_ref[...], shift, scale).astype(BF16)
    acc = None
    for c in range(N_FF_CHUNKS):
        cols = slice(c * FF_CHUNK, (c + 1) * FF_CHUNK)
        g = jnp.dot(h, wg_ref[:, cols], preferred_element_type=F32)
        u = jnp.dot(h, wu_ref[:, cols], preferred_element_type=F32)
        a = (g * jax.nn.sigmoid(g) * u).astype(BF16)
        d = jnp.dot(a, wd_ref[cols, :], preferred_element_type=F32)
        acc = d if acc is None else acc + d
        if c == 0 and after_first_chunk is not None:
            after_first_chunk()
    return x + (0.5 * (1.0 + gate)) * acc


def _ffn(x, mod, gain, wg, wu, wd, mod_base, mix=None):
    bsz, seq, _ = x.shape
    tile = pl.BlockSpec((1, ROW_TILE, D_MODEL), lambda b, s: (b, s, 0))
    half = pl.BlockSpec((1, ROW_TILE, SB_WIDTH), lambda b, s: (b, s, 0))
    mod_spec = pl.BlockSpec((1, N_MOD, D_MODEL), lambda b, s: (b, 0, 0))
    weights = [_resident((1, D_MODEL)), _resident((D_MODEL, D_FF)), _resident((D_MODEL, D_FF)),
               _resident((D_FF, D_MODEL))]
    if mix is None:
        in_specs, args = [tile, mod_spec] + weights, (x, mod, gain, wg, wu, wd)
    else:
        in_specs = [tile, mod_spec, half, half, _resident((D_MODEL, D_MODEL))] + weights
        args = (x, mod) + tuple(mix) + (gain, wg, wu, wd)
    return pl.pallas_call(
        functools.partial(_ffn_kernel, mod_base=mod_base, mix=mix is not None),
        grid=(bsz, seq // ROW_TILE),
        in_specs=in_specs,
        out_specs=tile,
        out_shape=jax.ShapeDtypeStruct(x.shape, F32),
        compiler_params=_params(2),
        name="ffn_mix" if mix is not None else "ffn",
    )(*args)


def _head_norm(t, gain, mult):
    lane = lax.broadcasted_iota(jnp.int32, (1, LANES), 1)
    first = lane < HEAD_DIM
    outs = []
    for j in range(SB_WIDTH // LANES):
        sl = t[:, j * LANES:(j + 1) * LANES]
        sq = sl * sl
        ms_a = jnp.sum(jnp.where(first, sq, 0.0), axis=-1, keepdims=True) * (1.0 / HEAD_DIM)
        ms_b = jnp.sum(jnp.where(first, 0.0, sq), axis=-1, keepdims=True) * (1.0 / HEAD_DIM)
        inv = jnp.where(first, lax.rsqrt(ms_a + EPS), lax.rsqrt(ms_b + EPS))
        y = sl * inv * gain
        if mult != 1.0:
            y = y * mult
        outs.append(y)
    return jnp.concatenate(outs, axis=-1)


def _ffn_proj_kernel(x_ref, mod_ref, gain1_ref, wg_ref, wu_ref, wd_ref,
                     gain_ref, win_ref, qg_ref, kg_ref, cw_ref, pw_ref, ps_ref,
                     xo_ref, q_ref, k_ref, v_ref, y_ref, hist_ref, lvl_ref):
    s = pl.program_id(1)
    m = mod_ref[0]
    rows = SUB_ROWS
    end = HIST_OFF + rows

    def project(x, t):
        sl = _sub_tiles()[t]
        xo_ref[0, sl, :] = x
        h = _norm_mod(x, gain_ref[...], m[3:4], m[4:5]).astype(BF16)

        q = jnp.dot(h, win_ref[:, 0:SB_WIDTH], preferred_element_type=F32)
        q_ref[0, sl, :] = _head_norm(q, qg_ref[...], HEAD_DIM ** -0.5 * LOG2E).astype(BF16)
        k = jnp.dot(h, win_ref[:, SB_WIDTH:2 * SB_WIDTH], preferred_element_type=F32)
        k_ref[0, sl, :] = _head_norm(k, kg_ref[...], 1.0).astype(BF16)
        v_ref[0, sl, :] = jnp.dot(h, win_ref[:, 2 * SB_WIDTH:3 * SB_WIDTH],
                                  preferred_element_type=F32).astype(BF16)

        base = 3 * SB_WIDTH
        gate_b = jnp.dot(h, win_ref[:, base:base + CONV_WIDTH], preferred_element_type=F32)
        gate_c = jnp.dot(h, win_ref[:, base + CONV_WIDTH:base + 2 * CONV_WIDTH], preferred_element_type=F32)
        u = jnp.dot(h, win_ref[:, base + 2 * CONV_WIDTH:base + 3 * CONV_WIDTH], preferred_element_type=F32)
        p = jnp.dot(h, win_ref[:, base + 3 * CONV_WIDTH:IN_WIDTH], preferred_element_type=F32)
        cu = gate_c * u

        def carry_halo():
            hist_ref[HIST_PAD:HIST_OFF, :] = hist_ref[rows + HIST_PAD:end, :]

        if t == 0:
            @pl.when(s == 0)
            def _():
                hist_ref[0:HIST_OFF, :] = jnp.zeros((HIST_OFF, CONV_WIDTH + POOL_WIDTH), F32)
                lvl_ref[:, 0:HIST_PAD, :] = jnp.zeros((3, HIST_PAD, POOL_WIDTH), F32)

            pl.when(s > 0)(carry_halo)
        else:
            carry_halo()

        hist_ref[HIST_OFF:end, 0:CONV_WIDTH] = cu
        hist_ref[HIST_OFF:end, CONV_WIDTH:] = p

        cw = cw_ref[...]
        conv = (cw[0:1] * hist_ref[HIST_OFF - 2:end - 2, 0:CONV_WIDTH]
                + cw[1:2] * hist_ref[HIST_OFF - 1:end - 1, 0:CONV_WIDTH]
                + cw[2:3] * cu)
        y_conv = gate_b * conv

        lo = HIST_PAD
        lvl_ref[0, lo:end, :] = hist_ref[lo:end, CONV_WIDTH:] + hist_ref[lo - 1:end - 1, CONV_WIDTH:]
        lvl_ref[1, lo:end, :] = lvl_ref[0, lo:end, :] + lvl_ref[0, lo - 2:end - 2, :]
        lvl_ref[2, lo:end, :] = lvl_ref[1, lo:end, :] + lvl_ref[1, lo - 4:end - 4, :]
        s2 = lvl_ref[0, HIST_OFF:end, :]
        s4 = lvl_ref[1, HIST_OFF:end, :]
        s8 = lvl_ref[2, HIST_OFF:end, :]
        s16 = s8 + lvl_ref[2, HIST_OFF - 8:end - 8, :]

        pos = (s * SUBS_PER_TILE + t) * rows
        t1 = (pos + 1 + lax.broadcasted_iota(jnp.int32, (rows, 1), 0)).astype(F32)
        inv = [1.0 / jnp.minimum(t1, float(w)) for w in POOL_WINDOWS]
        lane = lax.broadcasted_iota(jnp.int32, (1, LANES), 1)
        first = lane < POOL_WIDTH // len(POOL_WINDOWS)
        pooled = jnp.concatenate([
            jnp.where(first, s2[:, :LANES] * inv[0], s4[:, :LANES] * inv[1]),
            jnp.where(first, s8[:, LANES:] * inv[2], s16[:, LANES:] * inv[3]),
        ], axis=-1) - p
        y_pool = jnp.dot(pooled.astype(BF16), pw_ref[...], preferred_element_type=F32) * ps_ref[...]
        y_ref[0, sl, :] = jnp.concatenate([y_conv, y_pool], axis=-1).astype(BF16)

    ffn = functools.partial(_swiglu_residual, m=m, mod_base=0, gain_ref=gain1_ref,
                            wg_ref=wg_ref, wu_ref=wu_ref, wd_ref=wd_ref)
    prev = None
    for t, sl in enumerate(_sub_tiles()):
        hook = None if prev is None else functools.partial(project, prev, t - 1)
        prev = ffn(x_ref[0, sl, :], after_first_chunk=hook)
    project(prev, SUBS_PER_TILE - 1)


def _ffn_proj(x, mod, gain1, wg, wu, wd, gain, w_in, q_gain, k_gain, conv_w, pool_bd, pool_scale):
    bsz, seq, _ = x.shape
    tile = pl.BlockSpec((1, ROW_TILE, D_MODEL), lambda b, s: (b, s, 0))
    half = pl.BlockSpec((1, ROW_TILE, SB_WIDTH), lambda b, s: (b, s, 0))
    half_out = jax.ShapeDtypeStruct((bsz, seq, SB_WIDTH), BF16)
    return pl.pallas_call(
        _ffn_proj_kernel,
        grid=(bsz, seq // ROW_TILE),
        in_specs=[
            tile,
            pl.BlockSpec((1, N_MOD, D_MODEL), lambda b, s: (b, 0, 0)),
            _resident((1, D_MODEL)),
            _resident((D_MODEL, D_FF)),
            _resident((D_MODEL, D_FF)),
            _resident((D_FF, D_MODEL)),
            _resident((1, D_MODEL)),
            _resident((D_MODEL, IN_WIDTH)),
            _resident((1, LANES)),
            _resident((1, LANES)),
            _resident((CONV_K, CONV_WIDTH)),
            _resident((POOL_WIDTH, POOL_WIDTH)),
            _resident((1, POOL_WIDTH)),
        ],
        out_specs=[tile, half, half, half, half],
        out_shape=[jax.ShapeDtypeStruct(x.shape, F32), half_out, half_out, half_out, half_out],
        scratch_shapes=[
            pltpu.VMEM((HIST_OFF + SUB_ROWS, CONV_WIDTH + POOL_WIDTH), F32),
            pltpu.VMEM((3, HIST_OFF + SUB_ROWS, POOL_WIDTH), F32),
        ],
        compiler_params=_params(2),
        name="ffn_proj",
    )(x, mod, gain1, wg, wu, wd, gain, w_in, q_gain, k_gain, conv_w, pool_bd, pool_scale)


def _attn_kernel(q_ref, k_ref, v_ref, tri_ref, bias_ref, o_ref, acc_ref, run_ref):
    j = pl.program_id(2)
    tq, tk, tc = ATT_TILE, KEY_TILE, ATT_TILE
    n = 2 * tq
    lane = lax.broadcasted_iota(jnp.int32, (1, LANES), 1)
    first = lane < HEAD_DIM
    tri = tri_ref[...]

    def logits(g, hi):
        q2 = q_ref[0, pl.ds(pl.multiple_of(g * tq, tq), tq), :]
        zero = jnp.zeros_like(q2)
        qs = jnp.concatenate([jnp.where(first, q2, zero), jnp.where(first, zero, q2)], axis=0)
        lo = pl.multiple_of(jnp.maximum(hi - tk, 0), tc)
        kb = k_ref[0, pl.ds(lo, tk), :]
        return lax.dot_general(qs, kb, (((1,), (1,)), ((), ())), preferred_element_type=F32), lo

    def softplus_parts(w):
        sp = jnp.log(1.0 + jnp.exp2(w)) * LOG2E
        sp_bf = sp.astype(BF16)
        parts = [sp_bf[:, :tc], sp_bf[:, tc:]]
        if CUM_PASSES == 2:
            rest = (sp - sp_bf.astype(F32)).astype(BF16)
            parts += [rest[:, :tc], rest[:, tc:]]
        rs0 = jnp.sum(sp[:, :tc], axis=-1, keepdims=True)
        rs1 = jnp.sum(sp[:, tc:], axis=-1, keepdims=True)
        return jnp.concatenate(parts, axis=0), rs0, rs1, w - sp

    def suffix_in_tile(split):
        c = jnp.dot(split, tri, preferred_element_type=F32)
        if CUM_PASSES == 2:
            return c[0:n] + c[2 * n:3 * n], c[n:2 * n] + c[3 * n:4 * n]
        return c[0:n], c[n:2 * n]

    def masked_logits(g):
        w, lo = logits(g, (j * Q_GROUP + g + 1) * tq)
        if g == 0:
            col = lax.broadcasted_iota(jnp.int32, (n, tk), 1)
            row = lax.broadcasted_iota(jnp.int32, (n, tk), 0) & (tq - 1)
            w = jnp.where(col - row < j * Q_GROUP * tq - lo, jnp.minimum(w, Z_CLAMP), MASKED_LOGIT)
        else:
            bias = jnp.concatenate([bias_ref[...], bias_ref[...]], axis=0)
            w = jnp.concatenate([jnp.minimum(w[:, :tc], Z_CLAMP),
                                 jnp.minimum(w[:, tc:] + bias, Z_CLAMP)], axis=1)
        return w, lo

    def first_weights(parts, cum):
        _, _, rs1, part = parts
        a0 = jnp.exp2((part[:, :tc] - cum[0]) - rs1)
        a1 = jnp.exp2(part[:, tc:] - cum[1])
        return jnp.concatenate([a0, a1], axis=1).astype(BF16)

    def first_output(g, lo, parts, a):
        vb = v_ref[0, pl.ds(lo, tk), :]
        acc_ref[g] = jnp.dot(a, vb, preferred_element_type=F32)
        run_ref[g] = jnp.broadcast_to(parts[1] + parts[2], (n, LANES))

    tiles = range(Q_GROUP)
    lows, parts, cums, weights = {}, {}, {}, {}
    for s in range(Q_GROUP + 2):
        if s < Q_GROUP:
            w, lows[s] = masked_logits(s)
        if 0 <= s - 1 < Q_GROUP:
            cums[s - 1] = suffix_in_tile(parts[s - 1][0])
        if 0 <= s - 2 < Q_GROUP:
            first_output(s - 2, lows[s - 2], parts[s - 2], weights[s - 2])
        if s < Q_GROUP:
            parts[s] = softplus_parts(w)
        if 0 <= s - 1 < Q_GROUP:
            weights[s - 1] = first_weights(parts[s - 1], cums[s - 1])

    def sweep(g, hi):
        w, lo = logits(g, hi)
        col = lax.broadcasted_iota(jnp.int32, (n, tk), 1)
        w = jnp.where(col < hi - lo, jnp.minimum(w, Z_CLAMP), MASKED_LOGIT)
        split, rs0, rs1, part = softplus_parts(w)
        cum0, cum1 = suffix_in_tile(split)
        run = run_ref[g]
        off0 = run + rs1
        a0 = jnp.exp2((part[:, :tc] - cum0) - jnp.concatenate([off0, off0], axis=1))
        a1 = jnp.exp2((part[:, tc:] - cum1) - jnp.concatenate([run, run], axis=1))
        a = jnp.concatenate([a0, a1], axis=1).astype(BF16)
        vb = v_ref[0, pl.ds(lo, tk), :]
        acc_ref[g] += jnp.dot(a, vb, preferred_element_type=F32)
        run_ref[g] = off0 + rs0

    def finish_tile(g, carry):
        def more(state):
            hi, min_run = state
            return jnp.logical_and(hi > 0, min_run < RUN_DONE)

        def step(state):
            hi, _ = state
            sweep(g, hi)
            return jnp.maximum(hi - tk, 0), jnp.min(run_ref[g])

        hi1 = jnp.maximum((j * Q_GROUP + g + 1) * tq - tk, 0)
        lax.while_loop(more, step, (hi1, jnp.min(run_ref[g])))
        return carry

    pending = None
    for g in tiles:
        unswept = (j * Q_GROUP + g + 1) * tq > tk
        run_g = jnp.where(unswept, run_ref[g], RUN_DONE)
        pending = run_g if pending is None else jnp.minimum(pending, run_g)

    @pl.when(jnp.min(pending) < RUN_DONE)
    def _():
        lax.fori_loop(0, Q_GROUP, finish_tile, 0)

    for g in tiles:
        acc = acc_ref[g]
        o_ref[0, g * tq:(g + 1) * tq, :] = jnp.where(first, acc[0:tq], acc[tq:n]).astype(BF16)


def _attention(q, k, v, tri, causal_bias):
    bsz, seq, _ = q.shape
    n_pairs = SB_WIDTH // LANES
    rows = Q_GROUP * ATT_TILE
    q_tile = pl.BlockSpec((1, rows, LANES), lambda b, hp, j: (b, j, hp))
    kv_full = pl.BlockSpec((1, seq, LANES), lambda b, hp, j: (b, 0, hp))
    return pl.pallas_call(
        _attn_kernel,
        grid=(bsz, n_pairs, seq // rows),
        in_specs=[q_tile, kv_full, kv_full, _resident((ATT_TILE, ATT_TILE)), _resident((ATT_TILE, ATT_TILE))],
        out_specs=q_tile,
        out_shape=jax.ShapeDtypeStruct(q.shape, BF16),
        scratch_shapes=[pltpu.VMEM((Q_GROUP, 2 * ATT_TILE, LANES), F32),
                        pltpu.VMEM((Q_GROUP, 2 * ATT_TILE, LANES), F32)],
        compiler_params=_params(3),
        name="sb_attention",
    )(q, k, v, tri, causal_bias)


def _ffn_weights(w_gate, w_up, w_down):
    return w_gate.astype(BF16), w_up.astype(BF16), w_down.astype(BF16)


def _pool_block_diag(pool_w):
    g, c, _ = pool_w.shape
    eye = jnp.eye(g, dtype=pool_w.dtype)
    return (eye[:, None, :, None] * pool_w[:, :, None, :]).reshape(g * c, g * c).astype(BF16)


def kernel(x, c, w_ada, b_ada, ffn1_norm, ffn1_gate, ffn1_up, ffn1_down, mix_norm, w_in, q_norm,
           k_norm, conv_w, pool_w, pool_scale, w_out, ffn2_norm, ffn2_gate, ffn2_up, ffn2_down):
    n_layers = w_ada.shape[0]
    bsz = x.shape[0]
    mod_all = _ada_all_layers(c, w_ada, b_ada).reshape(n_layers, bsz, N_MOD, D_MODEL)
    lower = jnp.tril(jnp.ones((ATT_TILE, ATT_TILE), F32), -1)
    tri = lower.astype(BF16)
    causal_bias = (1.0 - lower) * MASKED_LOGIT

    for l in range(n_layers):
        mod = mod_all[l]
        x, q, k, v, y_cp = _ffn_proj(
            x, mod, ffn1_norm[l][None], *_ffn_weights(ffn1_gate[l], ffn1_up[l], ffn1_down[l]),
            mix_norm[l][None], w_in[l].astype(BF16),
            jnp.tile(q_norm[l], 2)[None], jnp.tile(k_norm[l], 2)[None],
            conv_w[l], _pool_block_diag(pool_w[l]), pool_scale[l][None])
        y_sb = _attention(q, k, v, tri, causal_bias)
        x = _ffn(x, mod, ffn2_norm[l][None], *_ffn_weights(ffn2_gate[l], ffn2_up[l], ffn2_down[l]),
                 mod_base=6, mix=(y_sb, y_cp, w_out[l].astype(BF16)))
    return x
```

```python
import functools
import math

import jax
import jax.numpy as jnp
from jax import lax
from jax.experimental import pallas as pl
from jax.experimental.pallas import tpu as pltpu

F32 = jnp.float32
BF16 = jnp.bfloat16

D_MODEL = 1024
HEAD_DIM = 64
SB_WIDTH = 512
CONV_WIDTH = 256
CONV_K = 3
POOL_WINDOWS = (2, 4, 8, 16)
POOL_WIDTH = 256
IN_WIDTH = 3 * SB_WIDTH + 3 * CONV_WIDTH + POOL_WIDTH
D_FF = 2816
N_MOD = 9
MIX_GATE_ROW = 5
EPS = 1e-6

LANES = 128
SUBLANES = 8
VMEM_LIMIT_BYTES = 56 * 1024 * 1024

ADA_TN = 1536
SUB_ROWS = 512
SUBS_PER_TILE = 2
ROW_TILE = SUB_ROWS * SUBS_PER_TILE
FF_CHUNK = 256
N_FF_CHUNKS = D_FF // FF_CHUNK
ATT_TILE = 256
Q_UNIT = 128
KEY_TILE = 512
Q_GROUP = 16
POOL_HALO = max(POOL_WINDOWS)
HIST_PAD = SUBLANES
HIST_OFF = HIST_PAD + POOL_HALO

LOG2E = math.log2(math.e)
LN2 = math.log(2.0)
Z_CLAMP = 126.0
MASKED_LOGIT = -1e30
RUN_DONE = 150.0


def _params(n_grid):
    return pltpu.CompilerParams(
        dimension_semantics=("arbitrary",) * n_grid,
        vmem_limit_bytes=VMEM_LIMIT_BYTES)


def _resident(shape, layer=None):
    zeros = (0,) * len(shape)
    if layer is None:
        return pl.BlockSpec(shape, lambda *_: zeros, pipeline_mode=pl.Buffered(1))
    return pl.BlockSpec((None,) + tuple(shape), lambda *_: (layer,) + zeros, pipeline_mode=pl.Buffered(1))


def _mod_spec(layer):
    return pl.BlockSpec((None, 1, N_MOD, D_MODEL), lambda b, s: (layer, b, 0, 0))


def _ada_kernel(c_ref, w_ref, b_ref, o_ref):
    c = c_ref[...]
    cond = (c * jax.nn.sigmoid(c)).astype(BF16)
    w = w_ref[0].astype(BF16)
    o_ref[0] = jnp.dot(cond, w, preferred_element_type=F32) + b_ref[0]


def _ada_all_layers(c, w_ada, b_ada):
    n_layers, _, n_out = w_ada.shape
    bsz = c.shape[0]
    return pl.pallas_call(
        _ada_kernel,
        grid=(n_layers, n_out // ADA_TN),
        in_specs=[
            pl.BlockSpec((bsz, D_MODEL), lambda l, j: (0, 0)),
            pl.BlockSpec((1, D_MODEL, ADA_TN), lambda l, j: (l, 0, j)),
            pl.BlockSpec((1, 1, ADA_TN), lambda l, j: (l, 0, j)),
        ],
        out_specs=pl.BlockSpec((1, bsz, ADA_TN), lambda l, j: (l, 0, j)),
        out_shape=jax.ShapeDtypeStruct((n_layers, bsz, n_out), F32),
        compiler_params=_params(2),
        name="adaln",
    )(c, w_ada, b_ada.reshape(n_layers, 1, n_out))


def _norm_mod(x, gain, shift, scale):
    ms = jnp.mean(x * x, axis=-1, keepdims=True)
    y = x * lax.rsqrt(ms + EPS) * gain
    return y * (1.0 + scale) + shift


def _ffn_kernel(*refs, mod_base, mix):
    if mix:
        x_ref, mod_ref, ysb_ref, ycp_ref, wo_ref, gain_ref, wg_ref, wu_ref, wd_ref, o_ref = refs
    else:
        x_ref, mod_ref, gain_ref, wg_ref, wu_ref, wd_ref, o_ref = refs
    m = mod_ref[0]
    xs = []
    for rows in _sub_tiles():
        x = x_ref[0, rows, :]
        if mix:
            mixed = (jnp.dot(ysb_ref[0, rows, :], wo_ref[0:SB_WIDTH, :], preferred_element_type=F32)
                     + jnp.dot(ycp_ref[0, rows, :], wo_ref[SB_WIDTH:, :], preferred_element_type=F32))
            x = x + (1.0 + m[MIX_GATE_ROW:MIX_GATE_ROW + 1]) * mixed
        xs.append(x)
    for rows, x in zip(_sub_tiles(), xs):
        o_ref[0, rows, :] = _swiglu_residual(x, m, mod_base, gain_ref, wg_ref, wu_ref, wd_ref)


def _sub_tiles():
    return [slice(t * SUB_ROWS, (t + 1) * SUB_ROWS) for t in range(SUBS_PER_TILE)]


def _swiglu_residual(x, m, mod_base, gain_ref, wg_ref, wu_ref, wd_ref, after_first_chunk=None):
    shift = m[mod_base:mod_base + 1]
    scale = m[mod_base + 1:mod_base + 2]
    gate = m[mod_base + 2:mod_base + 3]
    h = _norm_mod(x, gain_ref[...], shift, scale).astype(BF16)
    acc = None
    for c in range(N_FF_CHUNKS):
        cols = slice(c * FF_CHUNK, (c + 1) * FF_CHUNK)
        g = jnp.dot(h, wg_ref[:, cols], preferred_element_type=F32)
        u = jnp.dot(h, wu_ref[:, cols], preferred_element_type=F32)
        a = (g * jax.nn.sigmoid(g) * u).astype(BF16)
        d = jnp.dot(a, wd_ref[cols, :], preferred_element_type=F32)
        acc = d if acc is None else acc + d
        if c == 0 and after_first_chunk is not None:
            after_first_chunk()
    return x + (0.5 * (1.0 + gate)) * acc


def _ffn(layer, x, mod, gain, wg, wu, wd, mod_base, mix=None):
    bsz, seq, _ = x.shape
    tile = pl.BlockSpec((1, ROW_TILE, D_MODEL), lambda b, s: (b, s, 0))
    half = pl.BlockSpec((1, ROW_TILE, SB_WIDTH), lambda b, s: (b, s, 0))
    weights = [_resident((1, D_MODEL), layer), _resident((D_MODEL, D_FF), layer),
               _resident((D_MODEL, D_FF), layer), _resident((D_FF, D_MODEL), layer)]
    if mix is None:
        in_specs, args = [tile, _mod_spec(layer)] + weights, (x, mod, gain, wg, wu, wd)
    else:
        in_specs = [tile, _mod_spec(layer), half, half, _resident((D_MODEL, D_MODEL), layer)] + weights
        args = (x, mod) + tuple(mix) + (gain, wg, wu, wd)
    return pl.pallas_call(
        functools.partial(_ffn_kernel, mod_base=mod_base, mix=mix is not None),
        grid=(bsz, seq // ROW_TILE),
        in_specs=in_specs,
        out_specs=tile,
        out_shape=jax.ShapeDtypeStruct(x.shape, F32),
        compiler_params=_params(2),
        name="ffn_mix" if mix is not None else "ffn",
    )(*args)


def _head_norm(t, gain, mult):
    lane = lax.broadcasted_iota(jnp.int32, (1, LANES), 1)
    first = lane < HEAD_DIM
    outs = []
    for j in range(SB_WIDTH // LANES):
        sl = t[:, j * LANES:(j + 1) * LANES]
        sq = sl * sl
        ms_a = jnp.sum(jnp.where(first, sq, 0.0), axis=-1, keepdims=True) * (1.0 / HEAD_DIM)
        ms_b = jnp.sum(jnp.where(first, 0.0, sq), axis=-1, keepdims=True) * (1.0 / HEAD_DIM)
        inv = jnp.where(first, lax.rsqrt(ms_a + EPS), lax.rsqrt(ms_b + EPS))
        y = sl * inv * gain
        if mult != 1.0:
            y = y * mult
        outs.append(y)
    return jnp.concatenate(outs, axis=-1)


def _ffn_proj_kernel(x_ref, mod_ref, gain1_ref, wg_ref, wu_ref, wd_ref,
                     gain_ref, win_ref, qg_ref, kg_ref, cw_ref, pw_ref, ps_ref,
                     xo_ref, q_ref, k_ref, v_ref, y_ref, hist_ref, lvl_ref):
    s = pl.program_id(1)
    m = mod_ref[0]
    rows = SUB_ROWS
    end = HIST_OFF + rows

    def project(x, t):
        sl = _sub_tiles()[t]
        xo_ref[0, sl, :] = x
        h = _norm_mod(x, gain_ref[...], m[3:4], m[4:5]).astype(BF16)

        q = jnp.dot(h, win_ref[:, 0:SB_WIDTH], preferred_element_type=F32)
        q_ref[0, sl, :] = _head_norm(q, qg_ref[...], HEAD_DIM ** -0.5 * LOG2E).astype(BF16)
        k = jnp.dot(h, win_ref[:, SB_WIDTH:2 * SB_WIDTH], preferred_element_type=F32)
        k_ref[0, sl, :] = _head_norm(k, kg_ref[...], 1.0).astype(BF16)
        v_ref[0, sl, :] = jnp.dot(h, win_ref[:, 2 * SB_WIDTH:3 * SB_WIDTH],
                                  preferred_element_type=F32).astype(BF16)

        base = 3 * SB_WIDTH
        gate_b = jnp.dot(h, win_ref[:, base:base + CONV_WIDTH], preferred_element_type=F32)
        gate_c = jnp.dot(h, win_ref[:, base + CONV_WIDTH:base + 2 * CONV_WIDTH], preferred_element_type=F32)
        u = jnp.dot(h, win_ref[:, base + 2 * CONV_WIDTH:base + 3 * CONV_WIDTH], preferred_element_type=F32)
        p = jnp.dot(h, win_ref[:, base + 3 * CONV_WIDTH:IN_WIDTH], preferred_element_type=F32)
        cu = gate_c * u

        def carry_halo():
            hist_ref[HIST_PAD:HIST_OFF, :] = hist_ref[rows + HIST_PAD:end, :]

        if t == 0:
            @pl.when(s == 0)
            def _():
                hist_ref[0:HIST_OFF, :] = jnp.zeros((HIST_OFF, CONV_WIDTH + POOL_WIDTH), F32)
                lvl_ref[:, 0:HIST_PAD, :] = jnp.zeros((3, HIST_PAD, POOL_WIDTH), F32)

            pl.when(s > 0)(carry_halo)
        else:
            carry_halo()

        hist_ref[HIST_OFF:end, 0:CONV_WIDTH] = cu
        hist_ref[HIST_OFF:end, CONV_WIDTH:] = p

        cw = cw_ref[...]
        conv = (cw[0:1] * hist_ref[HIST_OFF - 2:end - 2, 0:CONV_WIDTH]
                + cw[1:2] * hist_ref[HIST_OFF - 1:end - 1, 0:CONV_WIDTH]
                + cw[2:3] * cu)
        y_conv = gate_b * conv

        lo = HIST_PAD
        lvl_ref[0, lo:end, :] = hist_ref[lo:end, CONV_WIDTH:] + hist_ref[lo - 1:end - 1, CONV_WIDTH:]
        lvl_ref[1, lo:end, :] = lvl_ref[0, lo:end, :] + lvl_ref[0, lo - 2:end - 2, :]
        lvl_ref[2, lo:end, :] = lvl_ref[1, lo:end, :] + lvl_ref[1, lo - 4:end - 4, :]
        s2 = lvl_ref[0, HIST_OFF:end, :]
        s4 = lvl_ref[1, HIST_OFF:end, :]
        s8 = lvl_ref[2, HIST_OFF:end, :]
        s16 = s8 + lvl_ref[2, HIST_OFF - 8:end - 8, :]

        pos = (s * SUBS_PER_TILE + t) * rows
        t1 = (pos + 1 + lax.broadcasted_iota(jnp.int32, (rows, 1), 0)).astype(F32)
        inv = [1.0 / jnp.minimum(t1, float(w)) for w in POOL_WINDOWS]
        lane = lax.broadcasted_iota(jnp.int32, (1, LANES), 1)
        first = lane < POOL_WIDTH // len(POOL_WINDOWS)
        pooled = jnp.concatenate([
            jnp.where(first, s2[:, :LANES] * inv[0], s4[:, :LANES] * inv[1]),
            jnp.where(first, s8[:, LANES:] * inv[2], s16[:, LANES:] * inv[3]),
        ], axis=-1) - p
        y_pool = jnp.dot(pooled.astype(BF16), pw_ref[...], preferred_element_type=F32) * ps_ref[...]
        y_ref[0, sl, :] = jnp.concatenate([y_conv, y_pool], axis=-1).astype(BF16)

    ffn = functools.partial(_swiglu_residual, m=m, mod_base=0, gain_ref=gain1_ref,
                            wg_ref=wg_ref, wu_ref=wu_ref, wd_ref=wd_ref)
    prev = None
    for t, sl in enumerate(_sub_tiles()):
        hook = None if prev is None else functools.partial(project, prev, t - 1)
        prev = ffn(x_ref[0, sl, :], after_first_chunk=hook)
    project(prev, SUBS_PER_TILE - 1)


def _ffn_proj(layer, x, mod, gain1, wg, wu, wd, gain, w_in, q_gain, k_gain, conv_w, pool_bd, pool_scale):
    bsz, seq, _ = x.shape
    tile = pl.BlockSpec((1, ROW_TILE, D_MODEL), lambda b, s: (b, s, 0))
    half = pl.BlockSpec((1, ROW_TILE, SB_WIDTH), lambda b, s: (b, s, 0))
    half_out = jax.ShapeDtypeStruct((bsz, seq, SB_WIDTH), BF16)
    return pl.pallas_call(
        _ffn_proj_kernel,
        grid=(bsz, seq // ROW_TILE),
        in_specs=[
            tile,
            _mod_spec(layer),
            _resident((1, D_MODEL), layer),
            _resident((D_MODEL, D_FF), layer),
            _resident((D_MODEL, D_FF), layer),
            _resident((D_FF, D_MODEL), layer),
            _resident((1, D_MODEL), layer),
            _resident((D_MODEL, IN_WIDTH), layer),
            _resident((1, LANES), layer),
            _resident((1, LANES), layer),
            _resident((CONV_K, CONV_WIDTH), layer),
            _resident((POOL_WIDTH, POOL_WIDTH), layer),
            _resident((1, POOL_WIDTH), layer),
        ],
        out_specs=[tile, half, half, half, half],
        out_shape=[jax.ShapeDtypeStruct(x.shape, F32), half_out, half_out, half_out, half_out],
        scratch_shapes=[
            pltpu.VMEM((HIST_OFF + SUB_ROWS, CONV_WIDTH + POOL_WIDTH), F32),
            pltpu.VMEM((3, HIST_OFF + SUB_ROWS, POOL_WIDTH), F32),
        ],
        compiler_params=_params(2),
        name="ffn_proj",
    )(x, mod, gain1, wg, wu, wd, gain, w_in, q_gain, k_gain, conv_w, pool_bd, pool_scale)


def _attn_kernel(q_ref, k_ref, v_ref, tri_ref, bias_ref, o_ref, acc_ref, run_ref):
    j = pl.program_id(2)
    tq, th, tk = Q_UNIT, ATT_TILE, KEY_TILE
    tw = th + tq
    n = 2 * tq
    lane = lax.broadcasted_iota(jnp.int32, (1, LANES), 1)
    first = lane < HEAD_DIM
    tri = tri_ref[...]
    tri_diag = tri_ref[0:tq, 0:tq]

    def logits(u, lo, width):
        q2 = q_ref[0, pl.ds(pl.multiple_of(u * tq, tq), tq), :]
        zero = jnp.zeros_like(q2)
        qs = jnp.concatenate([jnp.where(first, q2, zero), jnp.where(first, zero, q2)], axis=0)
        kb = k_ref[0, pl.ds(lo, width), :]
        return lax.dot_general(qs, kb, (((1,), (1,)), ((), ())), preferred_element_type=F32)

    def softplus_parts(w):
        sp = jnp.log(1.0 + jnp.exp2(w)) * LOG2E
        rs0 = jnp.sum(sp[:, :th], axis=-1, keepdims=True)
        rs1 = jnp.sum(sp[:, th:], axis=-1, keepdims=True)
        return sp.astype(BF16), rs0, rs1, w - sp

    def suffix(sp_bf, lower):
        return jnp.dot(sp_bf, lower, preferred_element_type=F32)

    def masked_logits(u):
        i = j * Q_GROUP + u
        lo = pl.multiple_of(jnp.maximum((i + 1) * tq - tw, 0), tq)
        w = logits(u, lo, tw)
        if u * tq < th:
            col = lax.broadcasted_iota(jnp.int32, (n, tw), 1)
            row = lax.broadcasted_iota(jnp.int32, (n, tw), 0) & (tq - 1)
            w = jnp.where(col - row < i * tq - lo, jnp.minimum(w, Z_CLAMP), MASKED_LOGIT)
        else:
            bias = jnp.concatenate([bias_ref[0:tq, 0:tq], bias_ref[0:tq, 0:tq]], axis=0)
            w = jnp.concatenate([jnp.minimum(w[:, :th], Z_CLAMP),
                                 jnp.minimum(w[:, th:] + bias, Z_CLAMP)], axis=1)
        return w, lo

    def first_weights(parts, cum):
        _, _, rs1, part = parts
        a0 = jnp.exp2((part[:, :th] - cum[0]) - rs1)
        a1 = jnp.exp2(part[:, th:] - cum[1])
        return jnp.concatenate([a0, a1], axis=1).astype(BF16)

    def first_output(u, lo, parts, a):
        vb = v_ref[0, pl.ds(lo, tw), :]
        acc_ref[u] = jnp.dot(a, vb, preferred_element_type=F32)
        run_ref[u] = jnp.broadcast_to(parts[1] + parts[2], (n, LANES))

    units = range(Q_GROUP)
    lows, parts, cums, weights = {}, {}, {}, {}
    for s in range(Q_GROUP + 2):
        if s < Q_GROUP:
            w, lows[s] = masked_logits(s)
        if 0 <= s - 1 < Q_GROUP:
            sp_bf = parts[s - 1][0]
            cums[s - 1] = (suffix(sp_bf[:, :th], tri), suffix(sp_bf[:, th:], tri_diag))
        if 0 <= s - 2 < Q_GROUP:
            first_output(s - 2, lows[s - 2], parts[s - 2], weights[s - 2])
        if s < Q_GROUP:
            parts[s] = softplus_parts(w)
        if 0 <= s - 1 < Q_GROUP:
            weights[s - 1] = first_weights(parts[s - 1], cums[s - 1])

    def sweep(u, hi):
        lo = pl.multiple_of(jnp.maximum(hi - tk, 0), tq)
        w = logits(u, lo, tk)
        col = lax.broadcasted_iota(jnp.int32, (n, tk), 1)
        w = jnp.where(col < hi - lo, jnp.minimum(w, Z_CLAMP), MASKED_LOGIT)
        sp_bf, rs0, rs1, part = softplus_parts(w)
        run = run_ref[u]
        off0 = run + rs1
        a0 = jnp.exp2((part[:, :th] - suffix(sp_bf[:, :th], tri)) - jnp.concatenate([off0, off0], axis=1))
        a1 = jnp.exp2((part[:, th:] - suffix(sp_bf[:, th:], tri)) - jnp.concatenate([run, run], axis=1))
        a = jnp.concatenate([a0, a1], axis=1).astype(BF16)
        vb = v_ref[0, pl.ds(lo, tk), :]
        acc_ref[u] += jnp.dot(a, vb, preferred_element_type=F32)
        run_ref[u] = off0 + rs0

    def first_lo(u):
        return jnp.maximum((j * Q_GROUP + u + 1) * tq - tw, 0)

    def finish_unit(u, carry):
        def more(state):
            hi, min_run = state
            return jnp.logical_and(hi > 0, min_run < RUN_DONE)

        def step(state):
            hi, _ = state
            sweep(u, hi)
            return jnp.maximum(hi - tk, 0), jnp.min(run_ref[u])

        lax.while_loop(more, step, (first_lo(u), jnp.min(run_ref[u])))
        return carry

    pending = None
    for u in units:
        run_u = jnp.where(first_lo(u) > 0, run_ref[u], RUN_DONE)
        pending = run_u if pending is None else jnp.minimum(pending, run_u)

    @pl.when(jnp.min(pending) < RUN_DONE)
    def _():
        lax.fori_loop(0, Q_GROUP, finish_unit, 0)

    for u in units:
        acc = acc_ref[u]
        o_ref[0, u * tq:(u + 1) * tq, :] = jnp.where(first, acc[0:tq], acc[tq:n]).astype(BF16)


def _attention(q, k, v, tri, causal_bias):
    bsz, seq, _ = q.shape
    n_pairs = SB_WIDTH // LANES
    rows = Q_GROUP * Q_UNIT
    q_tile = pl.BlockSpec((1, rows, LANES), lambda b, hp, j: (b, j, hp))
    kv_full = pl.BlockSpec((1, seq, LANES), lambda b, hp, j: (b, 0, hp))
    return pl.pallas_call(
        _attn_kernel,
        grid=(bsz, n_pairs, seq // rows),
        in_specs=[q_tile, kv_full, kv_full, _resident((ATT_TILE, ATT_TILE)), _resident((ATT_TILE, ATT_TILE))],
        out_specs=q_tile,
        out_shape=jax.ShapeDtypeStruct(q.shape, BF16),
        scratch_shapes=[pltpu.VMEM((Q_GROUP, 2 * Q_UNIT, LANES), F32),
                        pltpu.VMEM((Q_GROUP, 2 * Q_UNIT, LANES), F32)],
        compiler_params=_params(3),
        name="sb_attention",
    )(q, k, v, tri, causal_bias)


def _pool_block_diag(pool_w):
    n_layers, g, c, _ = pool_w.shape
    eye = jnp.eye(g, dtype=pool_w.dtype)
    blocks = eye[None, :, None, :, None] * pool_w[:, :, :, None, :]
    return blocks.reshape(n_layers, g * c, g * c).astype(BF16)


def kernel(x, c, w_ada, b_ada, ffn1_norm, ffn1_gate, ffn1_up, ffn1_down, mix_norm, w_in, q_norm,
           k_norm, conv_w, pool_w, pool_scale, w_out, ffn2_norm, ffn2_gate, ffn2_up, ffn2_down):
    n_layers = w_ada.shape[0]
    bsz = x.shape[0]
    mod = _ada_all_layers(c, w_ada, b_ada).reshape(n_layers, bsz, N_MOD, D_MODEL)
    lower = jnp.tril(jnp.ones((ATT_TILE, ATT_TILE), F32), -1)
    tri = lower.astype(BF16)
    causal_bias = (1.0 - lower) * MASKED_LOGIT

    ffn1 = (ffn1_norm[:, None], ffn1_gate.astype(BF16), ffn1_up.astype(BF16), ffn1_down.astype(BF16))
    ffn2 = (ffn2_norm[:, None], ffn2_gate.astype(BF16), ffn2_up.astype(BF16), ffn2_down.astype(BF16))
    mixer = (mix_norm[:, None], w_in.astype(BF16), jnp.tile(q_norm, (1, 2))[:, None],
             jnp.tile(k_norm, (1, 2))[:, None], conv_w, _pool_block_diag(pool_w), pool_scale[:, None])
    w_out = w_out.astype(BF16)

    for l in range(n_layers):
        x, q, k, v, y_cp = _ffn_proj(l, x, mod, *ffn1, *mixer)
        y_sb = _attention(q, k, v, tri, causal_bias)
        x = _ffn(l, x, mod, *ffn2, mod_base=6, mix=(y_sb, y_cp, w_out))
    return x
```

```python
import functools
import math

import jax
import jax.numpy as jnp
from jax import lax
from jax.experimental import pallas as pl
from jax.experimental.pallas import tpu as pltpu

F32 = jnp.float32
BF16 = jnp.bfloat16

D_MODEL = 1024
HEAD_DIM = 64
SB_WIDTH = 512
CONV_WIDTH = 256
CONV_K = 3
POOL_WINDOWS = (2, 4, 8, 16)
POOL_WIDTH = 256
IN_WIDTH = 3 * SB_WIDTH + 3 * CONV_WIDTH + POOL_WIDTH
D_FF = 2816
N_MOD = 9
MIX_GATE_ROW = 5
EPS = 1e-6

LANES = 128
SUBLANES = 8
VMEM_LIMIT_BYTES = 56 * 1024 * 1024

ADA_TN = 1536
SUB_ROWS = 512
SUBS_PER_TILE = 2
ROW_TILE = SUB_ROWS * SUBS_PER_TILE
FF_CHUNK = 256
N_FF_CHUNKS = D_FF // FF_CHUNK
ATT_TILE = 256
Q_UNIT = 128
KEY_TILE = 512
Q_GROUP = 32
POOL_HALO = max(POOL_WINDOWS)
HIST_PAD = SUBLANES
HIST_OFF = HIST_PAD + POOL_HALO

LOG2E = math.log2(math.e)
Z_CLAMP = 126.0
MASKED_LOGIT = -1e30
RUN_DONE = 150.0


def _params(n_grid):
    return pltpu.CompilerParams(
        dimension_semantics=("arbitrary",) * n_grid,
        vmem_limit_bytes=VMEM_LIMIT_BYTES)


def _resident(shape, layer=None):
    zeros = (0,) * len(shape)
    if layer is None:
        return pl.BlockSpec(shape, lambda *_: zeros, pipeline_mode=pl.Buffered(1))
    return pl.BlockSpec((None,) + tuple(shape), lambda *_: (layer,) + zeros, pipeline_mode=pl.Buffered(1))


def _mod_spec(layer):
    return pl.BlockSpec((None, 1, N_MOD, D_MODEL), lambda b, s: (layer, b, 0, 0))


def _ada_kernel(c_ref, w_ref, b_ref, o_ref):
    c = c_ref[...]
    cond = (c * jax.nn.sigmoid(c)).astype(BF16)
    w = w_ref[0].astype(BF16)
    o_ref[0] = jnp.dot(cond, w, preferred_element_type=F32) + b_ref[0]


def _ada_all_layers(c, w_ada, b_ada):
    n_layers, _, n_out = w_ada.shape
    bsz = c.shape[0]
    return pl.pallas_call(
        _ada_kernel,
        grid=(n_layers, n_out // ADA_TN),
        in_specs=[
            pl.BlockSpec((bsz, D_MODEL), lambda l, j: (0, 0)),
            pl.BlockSpec((1, D_MODEL, ADA_TN), lambda l, j: (l, 0, j)),
            pl.BlockSpec((1, 1, ADA_TN), lambda l, j: (l, 0, j)),
        ],
        out_specs=pl.BlockSpec((1, bsz, ADA_TN), lambda l, j: (l, 0, j)),
        out_shape=jax.ShapeDtypeStruct((n_layers, bsz, n_out), F32),
        compiler_params=_params(2),
        name="adaln",
    )(c, w_ada, b_ada.reshape(n_layers, 1, n_out))


def _norm_mod(x, gain, shift, scale):
    ms = jnp.mean(x * x, axis=-1, keepdims=True)
    y = x * lax.rsqrt(ms + EPS) * gain
    return y * (1.0 + scale) + shift


def _ffn_kernel(*refs, mod_base, mix):
    if mix:
        x_ref, mod_ref, ysb_ref, ycp_ref, wo_ref, gain_ref, wg_ref, wu_ref, wd_ref, o_ref = refs
    else:
        x_ref, mod_ref, gain_ref, wg_ref, wu_ref, wd_ref, o_ref = refs
    m = mod_ref[0]
    xs = []
    for rows in _sub_tiles():
        x = x_ref[0, rows, :]
        if mix:
            mixed = (jnp.dot(ysb_ref[0, rows, :], wo_ref[0:SB_WIDTH, :], preferred_element_type=F32)
                     + jnp.dot(ycp_ref[0, rows, :], wo_ref[SB_WIDTH:, :], preferred_element_type=F32))
            x = x + (1.0 + m[MIX_GATE_ROW:MIX_GATE_ROW + 1]) * mixed
        xs.append(x)
    for rows, x in zip(_sub_tiles(), xs):
        o_ref[0, rows, :] = _swiglu_residual(x, m, mod_base, gain_ref, wg_ref, wu_ref, wd_ref)


def _sub_tiles():
    return [slice(t * SUB_ROWS, (t + 1) * SUB_ROWS) for t in range(SUBS_PER_TILE)]


def _swiglu_residual(x, m, mod_base, gain_ref, wg_ref, wu_ref, wd_ref, after_first_chunk=None):
    shift = m[mod_base:mod_base + 1]
    scale = m[mod_base + 1:mod_base + 2]
    gate = m[mod_base + 2:mod_base + 3]
    h = _norm_mod(x, gain_ref[...], shift, scale).astype(BF16)
    acc = None
    for c in range(N_FF_CHUNKS):
        cols = slice(c * FF_CHUNK, (c + 1) * FF_CHUNK)
        g = jnp.dot(h, wg_ref[:, cols], preferred_element_type=F32)
        u = jnp.dot(h, wu_ref[:, cols], preferred_element_type=F32)
        a = (g * jax.nn.sigmoid(g) * u).astype(BF16)
        d = jnp.dot(a, wd_ref[cols, :], preferred_element_type=F32)
        acc = d if acc is None else acc + d
        if c == 0 and after_first_chunk is not None:
            after_first_chunk()
    return x + (0.5 * (1.0 + gate)) * acc


def _ffn(layer, x, mod, gain, wg, wu, wd, mod_base, mix=None):
    bsz, seq, _ = x.shape
    tile = pl.BlockSpec((1, ROW_TILE, D_MODEL), lambda b, s: (b, s, 0))
    half = pl.BlockSpec((1, ROW_TILE, SB_WIDTH), lambda b, s: (b, s, 0))
    weights = [_resident((1, D_MODEL), layer), _resident((D_MODEL, D_FF), layer),
               _resident((D_MODEL, D_FF), layer), _resident((D_FF, D_MODEL), layer)]
    if mix is None:
        in_specs, args = [tile, _mod_spec(layer)] + weights, (x, mod, gain, wg, wu, wd)
    else:
        in_specs = [tile, _mod_spec(layer), half, half, _resident((D_MODEL, D_MODEL), layer)] + weights
        args = (x, mod) + tuple(mix) + (gain, wg, wu, wd)
    return pl.pallas_call(
        functools.partial(_ffn_kernel, mod_base=mod_base, mix=mix is not None),
        grid=(bsz, seq // ROW_TILE),
        in_specs=in_specs,
        out_specs=tile,
        out_shape=jax.ShapeDtypeStruct(x.shape, F32),
        compiler_params=_params(2),
        name="ffn_mix" if mix is not None else "ffn",
    )(*args)


def _head_norm(t, gain, mult):
    lane = lax.broadcasted_iota(jnp.int32, (1, LANES), 1)
    first = lane < HEAD_DIM
    outs = []
    for j in range(SB_WIDTH // LANES):
        sl = t[:, j * LANES:(j + 1) * LANES]
        sq = sl * sl
        ms_a = jnp.sum(jnp.where(first, sq, 0.0), axis=-1, keepdims=True) * (1.0 / HEAD_DIM)
        ms_b = jnp.sum(jnp.where(first, 0.0, sq), axis=-1, keepdims=True) * (1.0 / HEAD_DIM)
        inv = jnp.where(first, lax.rsqrt(ms_a + EPS), lax.rsqrt(ms_b + EPS))
        y = sl * inv * gain
        if mult != 1.0:
            y = y * mult
        outs.append(y)
    return jnp.concatenate(outs, axis=-1)


def _ffn_proj_kernel(x_ref, mod_ref, gain1_ref, wg_ref, wu_ref, wd_ref,
                     gain_ref, win_ref, qg_ref, kg_ref, cw_ref, pw_ref, ps_ref,
                     xo_ref, q_ref, k_ref, v_ref, y_ref, hist_ref, lvl_ref):
    s = pl.program_id(1)
    m = mod_ref[0]
    rows = SUB_ROWS
    end = HIST_OFF + rows

    def project(x, t):
        sl = _sub_tiles()[t]
        xo_ref[0, sl, :] = x
        h = _norm_mod(x, gain_ref[...], m[3:4], m[4:5]).astype(BF16)

        q = jnp.dot(h, win_ref[:, 0:SB_WIDTH], preferred_element_type=F32)
        q_ref[0, sl, :] = _head_norm(q, qg_ref[...], HEAD_DIM ** -0.5 * LOG2E).astype(BF16)
        k = jnp.dot(h, win_ref[:, SB_WIDTH:2 * SB_WIDTH], preferred_element_type=F32)
        k_ref[0, sl, :] = _head_norm(k, kg_ref[...], 1.0).astype(BF16)
        v_ref[0, sl, :] = jnp.dot(h, win_ref[:, 2 * SB_WIDTH:3 * SB_WIDTH],
                                  preferred_element_type=F32).astype(BF16)

        base = 3 * SB_WIDTH
        gate_b = jnp.dot(h, win_ref[:, base:base + CONV_WIDTH], preferred_element_type=F32)
        gate_c = jnp.dot(h, win_ref[:, base + CONV_WIDTH:base + 2 * CONV_WIDTH], preferred_element_type=F32)
        u = jnp.dot(h, win_ref[:, base + 2 * CONV_WIDTH:base + 3 * CONV_WIDTH], preferred_element_type=F32)
        p = jnp.dot(h, win_ref[:, base + 3 * CONV_WIDTH:IN_WIDTH], preferred_element_type=F32)
        cu = gate_c * u

        def carry_halo():
            hist_ref[HIST_PAD:HIST_OFF, :] = hist_ref[rows + HIST_PAD:end, :]

        if t == 0:
            @pl.when(s == 0)
            def _():
                hist_ref[0:HIST_OFF, :] = jnp.zeros((HIST_OFF, CONV_WIDTH + POOL_WIDTH), F32)
                lvl_ref[:, 0:HIST_PAD, :] = jnp.zeros((3, HIST_PAD, POOL_WIDTH), F32)

            pl.when(s > 0)(carry_halo)
        else:
            carry_halo()

        hist_ref[HIST_OFF:end, 0:CONV_WIDTH] = cu
        hist_ref[HIST_OFF:end, CONV_WIDTH:] = p

        cw = cw_ref[...]
        conv = (cw[0:1] * hist_ref[HIST_OFF - 2:end - 2, 0:CONV_WIDTH]
                + cw[1:2] * hist_ref[HIST_OFF - 1:end - 1, 0:CONV_WIDTH]
                + cw[2:3] * cu)
        y_conv = gate_b * conv

        lo = HIST_PAD
        lvl_ref[0, lo:end, :] = hist_ref[lo:end, CONV_WIDTH:] + hist_ref[lo - 1:end - 1, CONV_WIDTH:]
        lvl_ref[1, lo:end, :] = lvl_ref[0, lo:end, :] + lvl_ref[0, lo - 2:end - 2, :]
        lvl_ref[2, lo:end, :] = lvl_ref[1, lo:end, :] + lvl_ref[1, lo - 4:end - 4, :]
        s2 = lvl_ref[0, HIST_OFF:end, :]
        s4 = lvl_ref[1, HIST_OFF:end, :]
        s8 = lvl_ref[2, HIST_OFF:end, :]
        s16 = s8 + lvl_ref[2, HIST_OFF - 8:end - 8, :]

        pos = (s * SUBS_PER_TILE + t) * rows
        t1 = (pos + 1 + lax.broadcasted_iota(jnp.int32, (rows, 1), 0)).astype(F32)
        inv = [1.0 / jnp.minimum(t1, float(w)) for w in POOL_WINDOWS]
        lane = lax.broadcasted_iota(jnp.int32, (1, LANES), 1)
        first = lane < POOL_WIDTH // len(POOL_WINDOWS)
        pooled = jnp.concatenate([
            jnp.where(first, s2[:, :LANES] * inv[0], s4[:, :LANES] * inv[1]),
            jnp.where(first, s8[:, LANES:] * inv[2], s16[:, LANES:] * inv[3]),
        ], axis=-1) - p
        y_pool = jnp.dot(pooled.astype(BF16), pw_ref[...], preferred_element_type=F32) * ps_ref[...]
        y_ref[0, sl, :] = jnp.concatenate([y_conv, y_pool], axis=-1).astype(BF16)

    ffn = functools.partial(_swiglu_residual, m=m, mod_base=0, gain_ref=gain1_ref,
                            wg_ref=wg_ref, wu_ref=wu_ref, wd_ref=wd_ref)
    prev = None
    for t, sl in enumerate(_sub_tiles()):
        hook = None if prev is None else functools.partial(project, prev, t - 1)
        prev = ffn(x_ref[0, sl, :], after_first_chunk=hook)
    project(prev, SUBS_PER_TILE - 1)


def _ffn_proj(layer, x, mod, gain1, wg, wu, wd, gain, w_in, q_gain, k_gain, conv_w, pool_bd, pool_scale):
    bsz, seq, _ = x.shape
    tile = pl.BlockSpec((1, ROW_TILE, D_MODEL), lambda b, s: (b, s, 0))
    half = pl.BlockSpec((1, ROW_TILE, SB_WIDTH), lambda b, s: (b, s, 0))
    half_out = jax.ShapeDtypeStruct((bsz, seq, SB_WIDTH), BF16)
    return pl.pallas_call(
        _ffn_proj_kernel,
        grid=(bsz, seq // ROW_TILE),
        in_specs=[
            tile,
            _mod_spec(layer),
            _resident((1, D_MODEL), layer),
            _resident((D_MODEL, D_FF), layer),
            _resident((D_MODEL, D_FF), layer),
            _resident((D_FF, D_MODEL), layer),
            _resident((1, D_MODEL), layer),
            _resident((D_MODEL, IN_WIDTH), layer),
            _resident((1, LANES), layer),
            _resident((1, LANES), layer),
            _resident((CONV_K, CONV_WIDTH), layer),
            _resident((POOL_WIDTH, POOL_WIDTH), layer),
            _resident((1, POOL_WIDTH), layer),
        ],
        out_specs=[tile, half, half, half, half],
        out_shape=[jax.ShapeDtypeStruct(x.shape, F32), half_out, half_out, half_out, half_out],
        scratch_shapes=[
            pltpu.VMEM((HIST_OFF + SUB_ROWS, CONV_WIDTH + POOL_WIDTH), F32),
            pltpu.VMEM((3, HIST_OFF + SUB_ROWS, POOL_WIDTH), F32),
        ],
        compiler_params=_params(2),
        name="ffn_proj",
    )(x, mod, gain1, wg, wu, wd, gain, w_in, q_gain, k_gain, conv_w, pool_bd, pool_scale)


def _attn_kernel(q_ref, k_ref, v_ref, tri_ref, bias_ref, o_ref, acc_ref, run_ref):
    j = pl.program_id(2)
    tq, th, tk = Q_UNIT, ATT_TILE, KEY_TILE
    tw = th + tq
    n = 2 * tq
    lane = lax.broadcasted_iota(jnp.int32, (1, LANES), 1)
    first = lane < HEAD_DIM
    tri = tri_ref[...]
    tri_diag = tri_ref[0:tq, 0:tq]

    def logits(u, lo, width):
        q2 = q_ref[0, pl.ds(pl.multiple_of(u * tq, tq), tq), :]
        zero = jnp.zeros_like(q2)
        qs = jnp.concatenate([jnp.where(first, q2, zero), jnp.where(first, zero, q2)], axis=0)
        kb = k_ref[0, pl.ds(lo, width), :]
        return lax.dot_general(qs, kb, (((1,), (1,)), ((), ())), preferred_element_type=F32)

    def softplus_parts(w):
        sp = jnp.log(1.0 + jnp.exp2(w)) * LOG2E
        rs0 = jnp.sum(sp[:, :th], axis=-1, keepdims=True)
        rs1 = jnp.sum(sp[:, th:], axis=-1, keepdims=True)
        return sp.astype(BF16), rs0, rs1, w - sp

    def suffix(sp_bf, lower):
        return jnp.dot(sp_bf, lower, preferred_element_type=F32)

    def masked_logits(u):
        i = j * Q_GROUP + u
        lo = pl.multiple_of(jnp.maximum((i + 1) * tq - tw, 0), tq)
        w = logits(u, lo, tw)
        if u * tq < th:
            col = lax.broadcasted_iota(jnp.int32, (n, tw), 1)
            row = lax.broadcasted_iota(jnp.int32, (n, tw), 0) & (tq - 1)
            w = jnp.where(col - row < i * tq - lo, jnp.minimum(w, Z_CLAMP), MASKED_LOGIT)
        else:
            bias = jnp.concatenate([bias_ref[0:tq, 0:tq], bias_ref[0:tq, 0:tq]], axis=0)
            w = jnp.concatenate([jnp.minimum(w[:, :th], Z_CLAMP),
                                 jnp.minimum(w[:, th:] + bias, Z_CLAMP)], axis=1)
        return w, lo

    def first_weights(parts, cum):
        _, _, rs1, part = parts
        a0 = jnp.exp2((part[:, :th] - cum[0]) - rs1)
        a1 = jnp.exp2(part[:, th:] - cum[1])
        return jnp.concatenate([a0, a1], axis=1).astype(BF16)

    def first_output(u, lo, parts, a):
        vb = v_ref[0, pl.ds(lo, tw), :]
        acc_ref[u] = jnp.dot(a, vb, preferred_element_type=F32)
        run_ref[u] = jnp.broadcast_to(parts[1] + parts[2], (n, LANES))

    units = range(Q_GROUP)
    lows, parts, cums, weights = {}, {}, {}, {}
    for s in range(Q_GROUP + 2):
        if s < Q_GROUP:
            w, lows[s] = masked_logits(s)
        if 0 <= s - 1 < Q_GROUP:
            sp_bf = parts[s - 1][0]
            cums[s - 1] = (suffix(sp_bf[:, :th], tri), suffix(sp_bf[:, th:], tri_diag))
        if 0 <= s - 2 < Q_GROUP:
            first_output(s - 2, lows[s - 2], parts[s - 2], weights[s - 2])
        if s < Q_GROUP:
            parts[s] = softplus_parts(w)
        if 0 <= s - 1 < Q_GROUP:
            weights[s - 1] = first_weights(parts[s - 1], cums[s - 1])

    def sweep(u, hi):
        lo = pl.multiple_of(jnp.maximum(hi - tk, 0), tq)
        w = logits(u, lo, tk)
        col = lax.broadcasted_iota(jnp.int32, (n, tk), 1)
        w = jnp.where(col < hi - lo, jnp.minimum(w, Z_CLAMP), MASKED_LOGIT)
        sp_bf, rs0, rs1, part = softplus_parts(w)
        run = run_ref[u]
        off0 = run + rs1
        a0 = jnp.exp2((part[:, :th] - suffix(sp_bf[:, :th], tri)) - jnp.concatenate([off0, off0], axis=1))
        a1 = jnp.exp2((part[:, th:] - suffix(sp_bf[:, th:], tri)) - jnp.concatenate([run, run], axis=1))
        a = jnp.concatenate([a0, a1], axis=1).astype(BF16)
        vb = v_ref[0, pl.ds(lo, tk), :]
        acc_ref[u] += jnp.dot(a, vb, preferred_element_type=F32)
        run_ref[u] = off0 + rs0

    def first_lo(u):
        return jnp.maximum((j * Q_GROUP + u + 1) * tq - tw, 0)

    def finish_unit(u, carry):
        def more(state):
            hi, min_run = state
            return jnp.logical_and(hi > 0, min_run < RUN_DONE)

        def step(state):
            hi, _ = state
            sweep(u, hi)
            return jnp.maximum(hi - tk, 0), jnp.min(run_ref[u])

        lax.while_loop(more, step, (first_lo(u), jnp.min(run_ref[u])))
        return carry

    pending = None
    for u in units:
        run_u = jnp.where(first_lo(u) > 0, run_ref[u], RUN_DONE)
        pending = run_u if pending is None else jnp.minimum(pending, run_u)

    @pl.when(jnp.min(pending) < RUN_DONE)
    def _():
        lax.fori_loop(0, Q_GROUP, finish_unit, 0)

    for u in units:
        acc = acc_ref[u]
        o_ref[0, u * tq:(u + 1) * tq, :] = jnp.where(first, acc[0:tq], acc[tq:n]).astype(BF16)


def _attention(q, k, v, tri, causal_bias):
    bsz, seq, _ = q.shape
    n_pairs = SB_WIDTH // LANES
    rows = Q_GROUP * Q_UNIT
    q_tile = pl.BlockSpec((1, rows, LANES), lambda b, hp, j: (b, j, hp))
    kv_full = pl.BlockSpec((1, seq, LANES), lambda b, hp, j: (b, 0, hp))
    return pl.pallas_call(
        _attn_kernel,
        grid=(bsz, n_pairs, seq // rows),
        in_specs=[q_tile, kv_full, kv_full, _resident((ATT_TILE, ATT_TILE)), _resident((ATT_TILE, ATT_TILE))],
        out_specs=q_tile,
        out_shape=jax.ShapeDtypeStruct(q.shape, BF16),
        scratch_shapes=[pltpu.VMEM((Q_GROUP, 2 * Q_UNIT, LANES), F32),
                        pltpu.VMEM((Q_GROUP, 2 * Q_UNIT, LANES), F32)],
        compiler_params=_params(3),
        name="sb_attention",
    )(q, k, v, tri, causal_bias)


def _pool_block_diag(pool_w):
    n_layers, g, c, _ = pool_w.shape
    eye = jnp.eye(g, dtype=pool_w.dtype)
    blocks = eye[None, :, None, :, None] * pool_w[:, :, :, None, :]
    return blocks.reshape(n_layers, g * c, g * c).astype(BF16)


def kernel(x, c, w_ada, b_ada, ffn1_norm, ffn1_gate, ffn1_up, ffn1_down, mix_norm, w_in, q_norm,
           k_norm, conv_w, pool_w, pool_scale, w_out, ffn2_norm, ffn2_gate, ffn2_up, ffn2_down):
    n_layers = w_ada.shape[0]
    bsz = x.shape[0]
    mod = _ada_all_layers(c, w_ada, b_ada).reshape(n_layers, bsz, N_MOD, D_MODEL)
    lower = jnp.tril(jnp.ones((ATT_TILE, ATT_TILE), F32), -1)
    tri = lower.astype(BF16)
    causal_bias = (1.0 - lower) * MASKED_LOGIT

    ffn1 = (ffn1_norm[:, None], ffn1_gate.astype(BF16), ffn1_up.astype(BF16), ffn1_down.astype(BF16))
    ffn2 = (ffn2_norm[:, None], ffn2_gate.astype(BF16), ffn2_up.astype(BF16), ffn2_down.astype(BF16))
    mixer = (mix_norm[:, None], w_in.astype(BF16), jnp.tile(q_norm, (1, 2))[:, None],
             jnp.tile(k_norm, (1, 2))[:, None], conv_w, _pool_block_diag(pool_w), pool_scale[:, None])
    w_out = w_out.astype(BF16)

    for l in range(n_layers):
        x, q, k, v, y_cp = _ffn_proj(l, x, mod, *ffn1, *mixer)
        y_sb = _attention(q, k, v, tri, causal_bias)
        x = _ffn(l, x, mod, *ffn2, mod_base=6, mix=(y_sb, y_cp, w_out))
    return x
```

```python
import functools
import math

import jax
import jax.numpy as jnp
from jax import lax
from jax.experimental import pallas as pl
from jax.experimental.pallas import tpu as pltpu

F32 = jnp.float32
BF16 = jnp.bfloat16

D_MODEL = 1024
HEAD_DIM = 64
SB_WIDTH = 512
CONV_WIDTH = 256
CONV_K = 3
POOL_WINDOWS = (2, 4, 8, 16)
POOL_WIDTH = 256
IN_WIDTH = 3 * SB_WIDTH + 3 * CONV_WIDTH + POOL_WIDTH
D_FF = 2816
N_MOD = 9
MIX_GATE_ROW = 5
EPS = 1e-6

LANES = 128
SUBLANES = 8
BF16_ROWS = 16
VMEM_LIMIT_BYTES = 56 * 1024 * 1024

ADA_TN = 1536
SUB_ROWS = 512
SUBS_PER_TILE = 2
ROW_TILE = SUB_ROWS * SUBS_PER_TILE
FF_CHUNK = 256
N_FF_CHUNKS = D_FF // FF_CHUNK
ATT_TILE = 256
Q_UNIT = 128
KEY_TILE = 512
Q_GROUP = 32
POOL_HALO = max(POOL_WINDOWS)
HIST_PAD = SUBLANES
HIST_OFF = HIST_PAD + POOL_HALO

LOG2E = math.log2(math.e)
Z_CLAMP = 126.0
MASKED_LOGIT = -1e30
RUN_DONE = 150.0


def _params(n_grid):
    return pltpu.CompilerParams(
        dimension_semantics=("arbitrary",) * n_grid,
        vmem_limit_bytes=VMEM_LIMIT_BYTES)


def _resident(shape, layer=None):
    zeros = (0,) * len(shape)
    if layer is None:
        return pl.BlockSpec(shape, lambda *_: zeros, pipeline_mode=pl.Buffered(1))
    return pl.BlockSpec((None,) + tuple(shape), lambda *_: (layer,) + zeros, pipeline_mode=pl.Buffered(1))


def _mod_spec(layer):
    return pl.BlockSpec((None, 1, N_MOD, D_MODEL), lambda b, s: (layer, b, 0, 0))


def _ada_kernel(c_ref, w_ref, b_ref, o_ref):
    c = c_ref[...]
    cond = (c * jax.nn.sigmoid(c)).astype(BF16)
    w = w_ref[0].astype(BF16)
    o_ref[0] = jnp.dot(cond, w, preferred_element_type=F32) + b_ref[0]


def _ada_all_layers(c, w_ada, b_ada):
    n_layers, _, n_out = w_ada.shape
    bsz = c.shape[0]
    return pl.pallas_call(
        _ada_kernel,
        grid=(n_layers, n_out // ADA_TN),
        in_specs=[
            pl.BlockSpec((bsz, D_MODEL), lambda l, j: (0, 0)),
            pl.BlockSpec((1, D_MODEL, ADA_TN), lambda l, j: (l, 0, j)),
            pl.BlockSpec((1, 1, ADA_TN), lambda l, j: (l, 0, j)),
        ],
        out_specs=pl.BlockSpec((1, bsz, ADA_TN), lambda l, j: (l, 0, j)),
        out_shape=jax.ShapeDtypeStruct((n_layers, bsz, n_out), F32),
        compiler_params=_params(2),
        name="adaln",
    )(c, w_ada, b_ada.reshape(n_layers, 1, n_out))


def _norm_mod(x, gain, shift, scale):
    ms = jnp.mean(x * x, axis=-1, keepdims=True)
    y = x * lax.rsqrt(ms + EPS) * gain
    return y * (1.0 + scale) + shift


def _ffn_kernel(*refs, mod_base, mix, n_cast):
    n_main = 9 if mix else 6
    main, cast_in, (o_ref, *cast_out) = refs[:n_main], refs[n_main:n_main + n_cast], refs[n_main + n_cast:]
    if mix:
        x_ref, mod_ref, ysb_ref, ycp_ref, wo_ref, gain_ref, wg_ref, wu_ref, wd_ref = main
    else:
        x_ref, mod_ref, gain_ref, wg_ref, wu_ref, wd_ref = main
    for src, dst in zip(cast_in, cast_out):
        dst[...] = src[...].astype(BF16)
    m = mod_ref[0]
    xs = []
    for rows in _sub_tiles():
        x = x_ref[0, rows, :]
        if mix:
            mixed = (jnp.dot(ysb_ref[0, rows, :], wo_ref[0:SB_WIDTH, :], preferred_element_type=F32)
                     + jnp.dot(ycp_ref[0, rows, :], wo_ref[SB_WIDTH:, :], preferred_element_type=F32))
            x = x + (1.0 + m[MIX_GATE_ROW:MIX_GATE_ROW + 1]) * mixed
        xs.append(x)
    for rows, x in zip(_sub_tiles(), xs):
        o_ref[0, rows, :] = _swiglu_residual(x, m, mod_base, gain_ref, wg_ref, wu_ref, wd_ref)


def _sub_tiles():
    return [slice(t * SUB_ROWS, (t + 1) * SUB_ROWS) for t in range(SUBS_PER_TILE)]


def _swiglu_residual(x, m, mod_base, gain_ref, wg_ref, wu_ref, wd_ref, after_first_chunk=None):
    shift = m[mod_base:mod_base + 1]
    scale = m[mod_base + 1:mod_base + 2]
    gate = m[mod_base + 2:mod_base + 3]
    h = _norm_mod(x, gain_ref[...], shift, scale).astype(BF16)
    acc = None
    for c in range(N_FF_CHUNKS):
        cols = slice(c * FF_CHUNK, (c + 1) * FF_CHUNK)
        g = jnp.dot(h, wg_ref[:, cols], preferred_element_type=F32)
        u = jnp.dot(h, wu_ref[:, cols], preferred_element_type=F32)
        a = (g * jax.nn.sigmoid(g) * u).astype(BF16)
        d = jnp.dot(a, wd_ref[cols, :], preferred_element_type=F32)
        acc = d if acc is None else acc + d
        if c == 0 and after_first_chunk is not None:
            after_first_chunk()
    return x + (0.5 * (1.0 + gate)) * acc


def _ffn(layer, x, mod, gain, wg, wu, wd, mod_base, mix=None, cast=None):
    bsz, seq, _ = x.shape
    n_s = seq // ROW_TILE
    tile = pl.BlockSpec((1, ROW_TILE, D_MODEL), lambda b, s: (b, s, 0))
    half = pl.BlockSpec((1, ROW_TILE, SB_WIDTH), lambda b, s: (b, s, 0))
    weights = [_resident((1, D_MODEL), layer), _resident((D_MODEL, D_FF)),
               _resident((D_MODEL, D_FF)), _resident((D_FF, D_MODEL))]
    if mix is None:
        in_specs, args = [tile, _mod_spec(layer)] + weights, (x, mod, gain, wg, wu, wd)
    else:
        in_specs = [tile, _mod_spec(layer), half, half, _resident((D_MODEL, D_MODEL))] + weights
        args = (x, mod) + tuple(mix) + (gain, wg, wu, wd)
    out_specs, out_shape = [tile], [jax.ShapeDtypeStruct(x.shape, F32)]
    n_cast = 0
    if cast is not None:
        next_layer, stacks = cast
        n_cast = len(stacks)
        slab = D_MODEL // (bsz * n_s)
        assert slab * bsz * n_s == D_MODEL and slab % BF16_ROWS == 0, (bsz, n_s)
        for w in stacks:
            cols = w.shape[-1]
            in_specs.append(pl.BlockSpec((None, slab, cols), lambda b, s: (next_layer, b * n_s + s, 0)))
            out_specs.append(pl.BlockSpec((slab, cols), lambda b, s: (b * n_s + s, 0)))
            out_shape.append(jax.ShapeDtypeStruct((D_MODEL, cols), BF16))
        args = args + tuple(stacks)
    outs = pl.pallas_call(
        functools.partial(_ffn_kernel, mod_base=mod_base, mix=mix is not None, n_cast=n_cast),
        grid=(bsz, n_s),
        in_specs=in_specs,
        out_specs=out_specs,
        out_shape=out_shape,
        compiler_params=_params(2),
        name="ffn_mix" if mix is not None else "ffn",
    )(*args)
    return outs[0], list(outs[1:])


def _head_norm(t, gain, mult):
    lane = lax.broadcasted_iota(jnp.int32, (1, LANES), 1)
    first = lane < HEAD_DIM
    outs = []
    for j in range(SB_WIDTH // LANES):
        sl = t[:, j * LANES:(j + 1) * LANES]
        sq = sl * sl
        ms_a = jnp.sum(jnp.where(first, sq, 0.0), axis=-1, keepdims=True) * (1.0 / HEAD_DIM)
        ms_b = jnp.sum(jnp.where(first, 0.0, sq), axis=-1, keepdims=True) * (1.0 / HEAD_DIM)
        inv = jnp.where(first, lax.rsqrt(ms_a + EPS), lax.rsqrt(ms_b + EPS))
        y = sl * inv * gain
        if mult != 1.0:
            y = y * mult
        outs.append(y)
    return jnp.concatenate(outs, axis=-1)


def _ffn_proj_kernel(x_ref, mod_ref, gain1_ref, wg_ref, wu_ref, wd_ref,
                     gain_ref, win_ref, qg_ref, kg_ref, cw_ref, pw_ref, ps_ref,
                     xo_ref, q_ref, k_ref, v_ref, y_ref, hist_ref, lvl_ref):
    s = pl.program_id(1)
    m = mod_ref[0]
    rows = SUB_ROWS
    end = HIST_OFF + rows

    def project(x, t):
        sl = _sub_tiles()[t]
        xo_ref[0, sl, :] = x
        h = _norm_mod(x, gain_ref[...], m[3:4], m[4:5]).astype(BF16)

        q = jnp.dot(h, win_ref[:, 0:SB_WIDTH], preferred_element_type=F32)
        q_ref[0, sl, :] = _head_norm(q, qg_ref[...], HEAD_DIM ** -0.5 * LOG2E).astype(BF16)
        k = jnp.dot(h, win_ref[:, SB_WIDTH:2 * SB_WIDTH], preferred_element_type=F32)
        k_ref[0, sl, :] = _head_norm(k, kg_ref[...], 1.0).astype(BF16)
        v_ref[0, sl, :] = jnp.dot(h, win_ref[:, 2 * SB_WIDTH:3 * SB_WIDTH],
                                  preferred_element_type=F32).astype(BF16)

        base = 3 * SB_WIDTH
        gate_b = jnp.dot(h, win_ref[:, base:base + CONV_WIDTH], preferred_element_type=F32)
        gate_c = jnp.dot(h, win_ref[:, base + CONV_WIDTH:base + 2 * CONV_WIDTH], preferred_element_type=F32)
        u = jnp.dot(h, win_ref[:, base + 2 * CONV_WIDTH:base + 3 * CONV_WIDTH], preferred_element_type=F32)
        p = jnp.dot(h, win_ref[:, base + 3 * CONV_WIDTH:IN_WIDTH], preferred_element_type=F32)
        cu = gate_c * u

        def carry_halo():
            hist_ref[HIST_PAD:HIST_OFF, :] = hist_ref[rows + HIST_PAD:end, :]

        if t == 0:
            @pl.when(s == 0)
            def _():
                hist_ref[0:HIST_OFF, :] = jnp.zeros((HIST_OFF, CONV_WIDTH + POOL_WIDTH), F32)
                lvl_ref[:, 0:HIST_PAD, :] = jnp.zeros((3, HIST_PAD, POOL_WIDTH), F32)

            pl.when(s > 0)(carry_halo)
        else:
            carry_halo()

        hist_ref[HIST_OFF:end, 0:CONV_WIDTH] = cu
        hist_ref[HIST_OFF:end, CONV_WIDTH:] = p

        cw = cw_ref[...]
        conv = (cw[0:1] * hist_ref[HIST_OFF - 2:end - 2, 0:CONV_WIDTH]
                + cw[1:2] * hist_ref[HIST_OFF - 1:end - 1, 0:CONV_WIDTH]
                + cw[2:3] * cu)
        y_conv = gate_b * conv

        lo = HIST_PAD
        lvl_ref[0, lo:end, :] = hist_ref[lo:end, CONV_WIDTH:] + hist_ref[lo - 1:end - 1, CONV_WIDTH:]
        lvl_ref[1, lo:end, :] = lvl_ref[0, lo:end, :] + lvl_ref[0, lo - 2:end - 2, :]
        lvl_ref[2, lo:end, :] = lvl_ref[1, lo:end, :] + lvl_ref[1, lo - 4:end - 4, :]
        s2 = lvl_ref[0, HIST_OFF:end, :]
        s4 = lvl_ref[1, HIST_OFF:end, :]
        s8 = lvl_ref[2, HIST_OFF:end, :]
        s16 = s8 + lvl_ref[2, HIST_OFF - 8:end - 8, :]

        pos = (s * SUBS_PER_TILE + t) * rows
        t1 = (pos + 1 + lax.broadcasted_iota(jnp.int32, (rows, 1), 0)).astype(F32)
        inv = [1.0 / jnp.minimum(t1, float(w)) for w in POOL_WINDOWS]
        lane = lax.broadcasted_iota(jnp.int32, (1, LANES), 1)
        first = lane < POOL_WIDTH // len(POOL_WINDOWS)
        pooled = jnp.concatenate([
            jnp.where(first, s2[:, :LANES] * inv[0], s4[:, :LANES] * inv[1]),
            jnp.where(first, s8[:, LANES:] * inv[2], s16[:, LANES:] * inv[3]),
        ], axis=-1) - p
        y_pool = jnp.dot(pooled.astype(BF16), pw_ref[...], preferred_element_type=F32) * ps_ref[...]
        y_ref[0, sl, :] = jnp.concatenate([y_conv, y_pool], axis=-1).astype(BF16)

    ffn = functools.partial(_swiglu_residual, m=m, mod_base=0, gain_ref=gain1_ref,
                            wg_ref=wg_ref, wu_ref=wu_ref, wd_ref=wd_ref)
    prev = None
    for t, sl in enumerate(_sub_tiles()):
        hook = None if prev is None else functools.partial(project, prev, t - 1)
        prev = ffn(x_ref[0, sl, :], after_first_chunk=hook)
    project(prev, SUBS_PER_TILE - 1)


def _ffn_proj(layer, x, mod, gain1, wg, wu, wd, gain, w_in, q_gain, k_gain, conv_w, pool_bd, pool_scale):
    bsz, seq, _ = x.shape
    tile = pl.BlockSpec((1, ROW_TILE, D_MODEL), lambda b, s: (b, s, 0))
    half = pl.BlockSpec((1, ROW_TILE, SB_WIDTH), lambda b, s: (b, s, 0))
    half_out = jax.ShapeDtypeStruct((bsz, seq, SB_WIDTH), BF16)
    return pl.pallas_call(
        _ffn_proj_kernel,
        grid=(bsz, seq // ROW_TILE),
        in_specs=[
            tile,
            _mod_spec(layer),
            _resident((1, D_MODEL), layer),
            _resident((D_MODEL, D_FF)),
            _resident((D_MODEL, D_FF)),
            _resident((D_FF, D_MODEL)),
            _resident((1, D_MODEL), layer),
            _resident((D_MODEL, IN_WIDTH)),
            _resident((1, LANES), layer),
            _resident((1, LANES), layer),
            _resident((CONV_K, CONV_WIDTH), layer),
            _resident((POOL_WIDTH, POOL_WIDTH), layer),
            _resident((1, POOL_WIDTH), layer),
        ],
        out_specs=[tile, half, half, half, half],
        out_shape=[jax.ShapeDtypeStruct(x.shape, F32), half_out, half_out, half_out, half_out],
        scratch_shapes=[
            pltpu.VMEM((HIST_OFF + SUB_ROWS, CONV_WIDTH + POOL_WIDTH), F32),
            pltpu.VMEM((3, HIST_OFF + SUB_ROWS, POOL_WIDTH), F32),
        ],
        compiler_params=_params(2),
        name="ffn_proj",
    )(x, mod, gain1, wg, wu, wd, gain, w_in, q_gain, k_gain, conv_w, pool_bd, pool_scale)


def _attn_kernel(q_ref, k_ref, v_ref, tri_ref, bias_ref, o_ref, acc_ref, run_ref):
    j = pl.program_id(2)
    tq, th, tk = Q_UNIT, ATT_TILE, KEY_TILE
    tw = th + tq
    n = 2 * tq
    lane = lax.broadcasted_iota(jnp.int32, (1, LANES), 1)
    first = lane < HEAD_DIM
    tri = tri_ref[...]
    tri_diag = tri_ref[0:tq, 0:tq]

    def logits(u, lo, width):
        q2 = q_ref[0, pl.ds(pl.multiple_of(u * tq, tq), tq), :]
        zero = jnp.zeros_like(q2)
        qs = jnp.concatenate([jnp.where(first, q2, zero), jnp.where(first, zero, q2)], axis=0)
        kb = k_ref[0, pl.ds(lo, width), :]
        return lax.dot_general(qs, kb, (((1,), (1,)), ((), ())), preferred_element_type=F32)

    def softplus_parts(w):
        sp = jnp.log(1.0 + jnp.exp2(w)) * LOG2E
        rs0 = jnp.sum(sp[:, :th], axis=-1, keepdims=True)
        rs1 = jnp.sum(sp[:, th:], axis=-1, keepdims=True)
        return sp.astype(BF16), rs0, rs1, w - sp

    def suffix(sp_bf, lower):
        return jnp.dot(sp_bf, lower, preferred_element_type=F32)

    def masked_logits(u):
        i = j * Q_GROUP + u
        lo = pl.multiple_of(jnp.maximum((i + 1) * tq - tw, 0), tq)
        w = logits(u, lo, tw)
        if u * tq < th:
            col = lax.broadcasted_iota(jnp.int32, (n, tw), 1)
            row = lax.broadcasted_iota(jnp.int32, (n, tw), 0) & (tq - 1)
            w = jnp.where(col - row < i * tq - lo, jnp.minimum(w, Z_CLAMP), MASKED_LOGIT)
        else:
            bias = jnp.concatenate([bias_ref[0:tq, 0:tq], bias_ref[0:tq, 0:tq]], axis=0)
            w = jnp.concatenate([jnp.minimum(w[:, :th], Z_CLAMP),
                                 jnp.minimum(w[:, th:] + bias, Z_CLAMP)], axis=1)
        return w, lo

    def first_weights(parts, cum):
        _, _, rs1, part = parts
        a0 = jnp.exp2((part[:, :th] - cum[0]) - rs1)
        a1 = jnp.exp2(part[:, th:] - cum[1])
        return jnp.concatenate([a0, a1], axis=1).astype(BF16)

    def first_output(u, lo, parts, a):
        vb = v_ref[0, pl.ds(lo, tw), :]
        acc_ref[u] = jnp.dot(a, vb, preferred_element_type=F32)
        run_ref[u] = jnp.broadcast_to(parts[1] + parts[2], (n, LANES))

    units = range(Q_GROUP)
    lows, parts, cums, weights = {}, {}, {}, {}
    for s in range(Q_GROUP + 2):
        if s < Q_GROUP:
            w, lows[s] = masked_logits(s)
        if 0 <= s - 1 < Q_GROUP:
            sp_bf = parts[s - 1][0]
            cums[s - 1] = (suffix(sp_bf[:, :th], tri), suffix(sp_bf[:, th:], tri_diag))
        if 0 <= s - 2 < Q_GROUP:
            first_output(s - 2, lows[s - 2], parts[s - 2], weights[s - 2])
        if s < Q_GROUP:
            parts[s] = softplus_parts(w)
        if 0 <= s - 1 < Q_GROUP:
            weights[s - 1] = first_weights(parts[s - 1], cums[s - 1])

    def sweep(u, hi):
        lo = pl.multiple_of(jnp.maximum(hi - tk, 0), tq)
        w = logits(u, lo, tk)
        col = lax.broadcasted_iota(jnp.int32, (n, tk), 1)
        w = jnp.where(col < hi - lo, jnp.minimum(w, Z_CLAMP), MASKED_LOGIT)
        sp_bf, rs0, rs1, part = softplus_parts(w)
        run = run_ref[u]
        off0 = run + rs1
        a0 = jnp.exp2((part[:, :th] - suffix(sp_bf[:, :th], tri)) - jnp.concatenate([off0, off0], axis=1))
        a1 = jnp.exp2((part[:, th:] - suffix(sp_bf[:, th:], tri)) - jnp.concatenate([run, run], axis=1))
        a = jnp.concatenate([a0, a1], axis=1).astype(BF16)
        vb = v_ref[0, pl.ds(lo, tk), :]
        acc_ref[u] += jnp.dot(a, vb, preferred_element_type=F32)
        run_ref[u] = off0 + rs0

    def first_lo(u):
        return jnp.maximum((j * Q_GROUP + u + 1) * tq - tw, 0)

    def finish_unit(u, carry):
        def more(state):
            hi, min_run = state
            return jnp.logical_and(hi > 0, min_run < RUN_DONE)

        def step(state):
            hi, _ = state
            sweep(u, hi)
            return jnp.maximum(hi - tk, 0), jnp.min(run_ref[u])

        lax.while_loop(more, step, (first_lo(u), jnp.min(run_ref[u])))
        return carry

    pending = None
    for u in units:
        run_u = jnp.where(first_lo(u) > 0, run_ref[u], RUN_DONE)
        pending = run_u if pending is None else jnp.minimum(pending, run_u)

    @pl.when(jnp.min(pending) < RUN_DONE)
    def _():
        lax.fori_loop(0, Q_GROUP, finish_unit, 0)

    for u in units:
        acc = acc_ref[u]
        o_ref[0, u * tq:(u + 1) * tq, :] = jnp.where(first, acc[0:tq], acc[tq:n]).astype(BF16)


def _attention(q, k, v, tri, causal_bias):
    bsz, seq, _ = q.shape
    n_pairs = SB_WIDTH // LANES
    rows = Q_GROUP * Q_UNIT
    q_tile = pl.BlockSpec((1, rows, LANES), lambda b, hp, j: (b, j, hp))
    kv_full = pl.BlockSpec((1, seq, LANES), lambda b, hp, j: (b, 0, hp))
    return pl.pallas_call(
        _attn_kernel,
        grid=(bsz, n_pairs, seq // rows),
        in_specs=[q_tile, kv_full, kv_full, _resident((ATT_TILE, ATT_TILE)), _resident((ATT_TILE, ATT_TILE))],
        out_specs=q_tile,
        out_shape=jax.ShapeDtypeStruct(q.shape, BF16),
        scratch_shapes=[pltpu.VMEM((Q_GROUP, 2 * Q_UNIT, LANES), F32),
                        pltpu.VMEM((Q_GROUP, 2 * Q_UNIT, LANES), F32)],
        compiler_params=_params(3),
        name="sb_attention",
    )(q, k, v, tri, causal_bias)


def _pool_block_diag(pool_w):
    n_layers, g, c, _ = pool_w.shape
    eye = jnp.eye(g, dtype=pool_w.dtype)
    blocks = eye[None, :, None, :, None] * pool_w[:, :, :, None, :]
    return blocks.reshape(n_layers, g * c, g * c).astype(BF16)


def kernel(x, c, w_ada, b_ada, ffn1_norm, ffn1_gate, ffn1_up, ffn1_down, mix_norm, w_in, q_norm,
           k_norm, conv_w, pool_w, pool_scale, w_out, ffn2_norm, ffn2_gate, ffn2_up, ffn2_down):
    n_layers = w_ada.shape[0]
    bsz = x.shape[0]
    mod = _ada_all_layers(c, w_ada, b_ada).reshape(n_layers, bsz, N_MOD, D_MODEL)
    lower = jnp.tril(jnp.ones((ATT_TILE, ATT_TILE), F32), -1)
    tri = lower.astype(BF16)
    causal_bias = (1.0 - lower) * MASKED_LOGIT

    stacks = [w.reshape(n_layers, D_MODEL, -1) for w in
              (ffn1_gate, ffn1_up, ffn1_down, w_in, ffn2_gate, ffn2_up, ffn2_down, w_out)]
    shapes = [w.shape[1:] for w in (ffn1_gate, ffn1_up, ffn1_down, w_in, ffn2_gate, ffn2_up, ffn2_down, w_out)]
    weights = [w[0].astype(BF16) for w in stacks]
    small = (mix_norm[:, None], jnp.tile(q_norm, (1, 2))[:, None], jnp.tile(k_norm, (1, 2))[:, None],
             conv_w, _pool_block_diag(pool_w), pool_scale[:, None])

    for l in range(n_layers):
        g1, u1, d1, win, g2, u2, d2, wout = [w.reshape(shape) for w, shape in zip(weights, shapes)]
        x, q, k, v, y_cp = _ffn_proj(l, x, mod, ffn1_norm[:, None], g1, u1, d1, small[0], win, *small[1:])
        y_sb = _attention(q, k, v, tri, causal_bias)
        cast = (l + 1, stacks) if l + 1 < n_layers else None
        x, weights = _ffn(l, x, mod, ffn2_norm[:, None], g2, u2, d2, mod_base=6, mix=(y_sb, y_cp, wout), cast=cast)
    return x
```

```python
import functools
import math

import jax
import jax.numpy as jnp
from jax import lax
from jax.experimental import pallas as pl
from jax.experimental.pallas import tpu as pltpu

F32 = jnp.float32
BF16 = jnp.bfloat16

D_MODEL = 1024
HEAD_DIM = 64
SB_WIDTH = 512
CONV_WIDTH = 256
CONV_K = 3
POOL_WINDOWS = (2, 4, 8, 16)
POOL_WIDTH = 256
IN_WIDTH = 3 * SB_WIDTH + 3 * CONV_WIDTH + POOL_WIDTH
D_FF = 2816
N_MOD = 9
MIX_GATE_ROW = 5
EPS = 1e-6

LANES = 128
SUBLANES = 8
VMEM_LIMIT_BYTES = 56 * 1024 * 1024

ADA_TN = 1536
SUB_ROWS = 512
SUBS_PER_TILE = 2
ROW_TILE = SUB_ROWS * SUBS_PER_TILE
FF_CHUNK = 256
N_FF_CHUNKS = D_FF // FF_CHUNK
ATT_TILE = 256
Q_UNIT = 128
KEY_TILE = 512
Q_GROUP = 32
POOL_HALO = max(POOL_WINDOWS)
HIST_PAD = SUBLANES
HIST_OFF = HIST_PAD + POOL_HALO

LOG2E = math.log2(math.e)
Z_CLAMP = 126.0
MASKED_LOGIT = -1e30
RUN_DONE = 150.0


def _params(n_grid):
    return pltpu.CompilerParams(
        dimension_semantics=("arbitrary",) * n_grid,
        vmem_limit_bytes=VMEM_LIMIT_BYTES)


def _resident(shape, layer=None):
    zeros = (0,) * len(shape)
    if layer is None:
        return pl.BlockSpec(shape, lambda *_: zeros, pipeline_mode=pl.Buffered(1))
    return pl.BlockSpec((None,) + tuple(shape), lambda *_: (layer,) + zeros, pipeline_mode=pl.Buffered(1))


def _mod_spec(layer):
    return pl.BlockSpec((None, 1, N_MOD, D_MODEL), lambda b, s: (layer, b, 0, 0))


def _ada_kernel(c_ref, w_ref, b_ref, o_ref):
    c = c_ref[...]
    cond = (c * jax.nn.sigmoid(c)).astype(BF16)
    w = w_ref[0].astype(BF16)
    o_ref[0] = jnp.dot(cond, w, preferred_element_type=F32) + b_ref[0]


def _ada_all_layers(c, w_ada, b_ada):
    n_layers, _, n_out = w_ada.shape
    bsz = c.shape[0]
    return pl.pallas_call(
        _ada_kernel,
        grid=(n_layers, n_out // ADA_TN),
        in_specs=[
            pl.BlockSpec((bsz, D_MODEL), lambda l, j: (0, 0)),
            pl.BlockSpec((1, D_MODEL, ADA_TN), lambda l, j: (l, 0, j)),
            pl.BlockSpec((1, 1, ADA_TN), lambda l, j: (l, 0, j)),
        ],
        out_specs=pl.BlockSpec((1, bsz, ADA_TN), lambda l, j: (l, 0, j)),
        out_shape=jax.ShapeDtypeStruct((n_layers, bsz, n_out), F32),
        compiler_params=_params(2),
        name="adaln",
    )(c, w_ada, b_ada.reshape(n_layers, 1, n_out))


def _norm_mod(x, gain, shift, scale):
    ms = jnp.mean(x * x, axis=-1, keepdims=True)
    y = x * lax.rsqrt(ms + EPS) * gain
    return y * (1.0 + scale) + shift


def _ffn_kernel(*refs, mod_base, mix):
    if mix:
        x_ref, mod_ref, ysb_ref, ycp_ref, wo_ref, gain_ref, wg_ref, wu_ref, wd_ref, o_ref = refs
    else:
        x_ref, mod_ref, gain_ref, wg_ref, wu_ref, wd_ref, o_ref = refs
    m = mod_ref[0]
    xs = []
    for rows in _sub_tiles():
        x = x_ref[0, rows, :]
        if mix:
            mixed = (jnp.dot(ysb_ref[0, rows, :], wo_ref[0:SB_WIDTH, :], preferred_element_type=F32)
                     + jnp.dot(ycp_ref[0, rows, :], wo_ref[SB_WIDTH:, :], preferred_element_type=F32))
            x = x + (1.0 + m[MIX_GATE_ROW:MIX_GATE_ROW + 1]) * mixed
        xs.append(x)
    for rows, x in zip(_sub_tiles(), xs):
        o_ref[0, rows, :] = _swiglu_residual(x, m, mod_base, gain_ref, wg_ref, wu_ref, wd_ref)


def _sub_tiles():
    return [slice(t * SUB_ROWS, (t + 1) * SUB_ROWS) for t in range(SUBS_PER_TILE)]


def _swiglu_residual(x, m, mod_base, gain_ref, wg_ref, wu_ref, wd_ref, after_first_chunk=None):
    shift = m[mod_base:mod_base + 1]
    scale = m[mod_base + 1:mod_base + 2]
    gate = m[mod_base + 2:mod_base + 3]
    h = _norm_mod(x, gain_ref[...], shift, scale).astype(BF16)
    hidden = []
    for c in range(N_FF_CHUNKS):
        cols = slice(c * FF_CHUNK, (c + 1) * FF_CHUNK)
        g = jnp.dot(h, wg_ref[:, cols], preferred_element_type=F32)
        u = jnp.dot(h, wu_ref[:, cols], preferred_element_type=F32)
        hidden.append((g * jax.nn.sigmoid(g) * u).astype(BF16))
        if c == 0 and after_first_chunk is not None:
            after_first_chunk()
    acc = jnp.dot(jnp.concatenate(hidden, axis=-1), wd_ref[...], preferred_element_type=F32)
    return x + (0.5 * (1.0 + gate)) * acc


def _ffn(layer, x, mod, gain, wg, wu, wd, mod_base, mix=None):
    bsz, seq, _ = x.shape
    tile = pl.BlockSpec((1, ROW_TILE, D_MODEL), lambda b, s: (b, s, 0))
    half = pl.BlockSpec((1, ROW_TILE, SB_WIDTH), lambda b, s: (b, s, 0))
    weights = [_resident((1, D_MODEL), layer), _resident((D_MODEL, D_FF), layer),
               _resident((D_MODEL, D_FF), layer), _resident((D_FF, D_MODEL), layer)]
    if mix is None:
        in_specs, args = [tile, _mod_spec(layer)] + weights, (x, mod, gain, wg, wu, wd)
    else:
        in_specs = [tile, _mod_spec(layer), half, half, _resident((D_MODEL, D_MODEL), layer)] + weights
        args = (x, mod) + tuple(mix) + (gain, wg, wu, wd)
    return pl.pallas_call(
        functools.partial(_ffn_kernel, mod_base=mod_base, mix=mix is not None),
        grid=(bsz, seq // ROW_TILE),
        in_specs=in_specs,
        out_specs=tile,
        out_shape=jax.ShapeDtypeStruct(x.shape, F32),
        compiler_params=_params(2),
        name="ffn_mix" if mix is not None else "ffn",
    )(*args)


def _head_norm(t, gain, mult):
    lane = lax.broadcasted_iota(jnp.int32, (1, LANES), 1)
    first = lane < HEAD_DIM
    outs = []
    for j in range(SB_WIDTH // LANES):
        sl = t[:, j * LANES:(j + 1) * LANES]
        sq = sl * sl
        ms_a = jnp.sum(jnp.where(first, sq, 0.0), axis=-1, keepdims=True) * (1.0 / HEAD_DIM)
        ms_b = jnp.sum(jnp.where(first, 0.0, sq), axis=-1, keepdims=True) * (1.0 / HEAD_DIM)
        inv = jnp.where(first, lax.rsqrt(ms_a + EPS), lax.rsqrt(ms_b + EPS))
        y = sl * inv * gain
        if mult != 1.0:
            y = y * mult
        outs.append(y)
    return jnp.concatenate(outs, axis=-1)


def _ffn_proj_kernel(x_ref, mod_ref, gain1_ref, wg_ref, wu_ref, wd_ref,
                     gain_ref, win_ref, qg_ref, kg_ref, cw_ref, pw_ref, ps_ref,
                     xo_ref, q_ref, k_ref, v_ref, y_ref, hist_ref, lvl_ref):
    s = pl.program_id(1)
    m = mod_ref[0]
    rows = SUB_ROWS
    end = HIST_OFF + rows

    def project(x, t):
        sl = _sub_tiles()[t]
        xo_ref[0, sl, :] = x
        h = _norm_mod(x, gain_ref[...], m[3:4], m[4:5]).astype(BF16)

        q = jnp.dot(h, win_ref[:, 0:SB_WIDTH], preferred_element_type=F32)
        q_ref[0, sl, :] = _head_norm(q, qg_ref[...], HEAD_DIM ** -0.5 * LOG2E).astype(BF16)
        k = jnp.dot(h, win_ref[:, SB_WIDTH:2 * SB_WIDTH], preferred_element_type=F32)
        k_ref[0, sl, :] = _head_norm(k, kg_ref[...], 1.0).astype(BF16)
        v_ref[0, sl, :] = jnp.dot(h, win_ref[:, 2 * SB_WIDTH:3 * SB_WIDTH],
                                  preferred_element_type=F32).astype(BF16)

        base = 3 * SB_WIDTH
        gate_b = jnp.dot(h, win_ref[:, base:base + CONV_WIDTH], preferred_element_type=F32)
        gate_c = jnp.dot(h, win_ref[:, base + CONV_WIDTH:base + 2 * CONV_WIDTH], preferred_element_type=F32)
        u = jnp.dot(h, win_ref[:, base + 2 * CONV_WIDTH:base + 3 * CONV_WIDTH], preferred_element_type=F32)
        p = jnp.dot(h, win_ref[:, base + 3 * CONV_WIDTH:IN_WIDTH], preferred_element_type=F32)
        cu = gate_c * u

        def carry_halo():
            hist_ref[HIST_PAD:HIST_OFF, :] = hist_ref[rows + HIST_PAD:end, :]

        if t == 0:
            @pl.when(s == 0)
            def _():
                hist_ref[0:HIST_OFF, :] = jnp.zeros((HIST_OFF, CONV_WIDTH + POOL_WIDTH), F32)
                lvl_ref[:, 0:HIST_PAD, :] = jnp.zeros((3, HIST_PAD, POOL_WIDTH), F32)

            pl.when(s > 0)(carry_halo)
        else:
            carry_halo()

        hist_ref[HIST_OFF:end, 0:CONV_WIDTH] = cu
        hist_ref[HIST_OFF:end, CONV_WIDTH:] = p

        cw = cw_ref[...]
        conv = (cw[0:1] * hist_ref[HIST_OFF - 2:end - 2, 0:CONV_WIDTH]
                + cw[1:2] * hist_ref[HIST_OFF - 1:end - 1, 0:CONV_WIDTH]
                + cw[2:3] * cu)
        y_conv = gate_b * conv

        lo = HIST_PAD
        lvl_ref[0, lo:end, :] = hist_ref[lo:end, CONV_WIDTH:] + hist_ref[lo - 1:end - 1, CONV_WIDTH:]
        lvl_ref[1, lo:end, :] = lvl_ref[0, lo:end, :] + lvl_ref[0, lo - 2:end - 2, :]
        lvl_ref[2, lo:end, :] = lvl_ref[1, lo:end, :] + lvl_ref[1, lo - 4:end - 4, :]
        s2 = lvl_ref[0, HIST_OFF:end, :]
        s4 = lvl_ref[1, HIST_OFF:end, :]
        s8 = lvl_ref[2, HIST_OFF:end, :]
        s16 = s8 + lvl_ref[2, HIST_OFF - 8:end - 8, :]

        pos = (s * SUBS_PER_TILE + t) * rows
        t1 = (pos + 1 + lax.broadcasted_iota(jnp.int32, (rows, 1), 0)).astype(F32)
        inv = [1.0 / jnp.minimum(t1, float(w)) for w in POOL_WINDOWS]
        lane = lax.broadcasted_iota(jnp.int32, (1, LANES), 1)
        first = lane < POOL_WIDTH // len(POOL_WINDOWS)
        pooled = jnp.concatenate([
            jnp.where(first, s2[:, :LANES] * inv[0], s4[:, :LANES] * inv[1]),
            jnp.where(first, s8[:, LANES:] * inv[2], s16[:, LANES:] * inv[3]),
        ], axis=-1) - p
        y_pool = jnp.dot(pooled.astype(BF16), pw_ref[...], preferred_element_type=F32) * ps_ref[...]
        y_ref[0, sl, :] = jnp.concatenate([y_conv, y_pool], axis=-1).astype(BF16)

    ffn = functools.partial(_swiglu_residual, m=m, mod_base=0, gain_ref=gain1_ref,
                            wg_ref=wg_ref, wu_ref=wu_ref, wd_ref=wd_ref)
    prev = None
    for t, sl in enumerate(_sub_tiles()):
        hook = None if prev is None else functools.partial(project, prev, t - 1)
        prev = ffn(x_ref[0, sl, :], after_first_chunk=hook)
    project(prev, SUBS_PER_TILE - 1)


def _ffn_proj(layer, x, mod, gain1, wg, wu, wd, gain, w_in, q_gain, k_gain, conv_w, pool_bd, pool_scale):
    bsz, seq, _ = x.shape
    tile = pl.BlockSpec((1, ROW_TILE, D_MODEL), lambda b, s: (b, s, 0))
    half = pl.BlockSpec((1, ROW_TILE, SB_WIDTH), lambda b, s: (b, s, 0))
    half_out = jax.ShapeDtypeStruct((bsz, seq, SB_WIDTH), BF16)
    return pl.pallas_call(
        _ffn_proj_kernel,
        grid=(bsz, seq // ROW_TILE),
        in_specs=[
            tile,
            _mod_spec(layer),
            _resident((1, D_MODEL), layer),
            _resident((D_MODEL, D_FF), layer),
            _resident((D_MODEL, D_FF), layer),
            _resident((D_FF, D_MODEL), layer),
            _resident((1, D_MODEL), layer),
            _resident((D_MODEL, IN_WIDTH), layer),
            _resident((1, LANES), layer),
            _resident((1, LANES), layer),
            _resident((CONV_K, CONV_WIDTH), layer),
            _resident((POOL_WIDTH, POOL_WIDTH), layer),
            _resident((1, POOL_WIDTH), layer),
        ],
        out_specs=[tile, half, half, half, half],
        out_shape=[jax.ShapeDtypeStruct(x.shape, F32), half_out, half_out, half_out, half_out],
        scratch_shapes=[
            pltpu.VMEM((HIST_OFF + SUB_ROWS, CONV_WIDTH + POOL_WIDTH), F32),
            pltpu.VMEM((3, HIST_OFF + SUB_ROWS, POOL_WIDTH), F32),
        ],
        compiler_params=_params(2),
        name="ffn_proj",
    )(x, mod, gain1, wg, wu, wd, gain, w_in, q_gain, k_gain, conv_w, pool_bd, pool_scale)


def _attn_kernel(q_ref, k_ref, v_ref, tri_ref, bias_ref, o_ref, acc_ref, run_ref):
    j = pl.program_id(2)
    tq, th, tk = Q_UNIT, ATT_TILE, KEY_TILE
    tw = th + tq
    n = 2 * tq
    lane = lax.broadcasted_iota(jnp.int32, (1, LANES), 1)
    first = lane < HEAD_DIM
    tri = tri_ref[...]
    tri_diag = tri_ref[0:tq, 0:tq]

    def logits(u, lo, width):
        q2 = q_ref[0, pl.ds(pl.multiple_of(u * tq, tq), tq), :]
        zero = jnp.zeros_like(q2)
        qs = jnp.concatenate([jnp.where(first, q2, zero), jnp.where(first, zero, q2)], axis=0)
        kb = k_ref[0, pl.ds(lo, width), :]
        return lax.dot_general(qs, kb, (((1,), (1,)), ((), ())), preferred_element_type=F32)

    def softplus_parts(w):
        sp = jnp.log(1.0 + jnp.exp2(w)) * LOG2E
        rs0 = jnp.sum(sp[:, :th], axis=-1, keepdims=True)
        rs1 = jnp.sum(sp[:, th:], axis=-1, keepdims=True)
        return sp.astype(BF16), rs0, rs1, w - sp

    def suffix(sp_bf, lower):
        return jnp.dot(sp_bf, lower, preferred_element_type=F32)

    def masked_logits(u):
        i = j * Q_GROUP + u
        lo = pl.multiple_of(jnp.maximum((i + 1) * tq - tw, 0), tq)
        w = logits(u, lo, tw)
        if u * tq < th:
            col = lax.broadcasted_iota(jnp.int32, (n, tw), 1)
            row = lax.broadcasted_iota(jnp.int32, (n, tw), 0) & (tq - 1)
            w = jnp.where(col - row < i * tq - lo, jnp.minimum(w, Z_CLAMP), MASKED_LOGIT)
        else:
            bias = jnp.concatenate([bias_ref[0:tq, 0:tq], bias_ref[0:tq, 0:tq]], axis=0)
            w = jnp.concatenate([jnp.minimum(w[:, :th], Z_CLAMP),
                                 jnp.minimum(w[:, th:] + bias, Z_CLAMP)], axis=1)
        return w, lo

    def first_weights(parts, cum):
        _, _, rs1, part = parts
        a0 = jnp.exp2((part[:, :th] - cum[0]) - rs1)
        a1 = jnp.exp2(part[:, th:] - cum[1])
        return jnp.concatenate([a0, a1], axis=1).astype(BF16)

    def first_output(u, lo, parts, a):
        vb = v_ref[0, pl.ds(lo, tw), :]
        acc_ref[u] = jnp.dot(a, vb, preferred_element_type=F32)
        run_ref[u] = jnp.broadcast_to(parts[1] + parts[2], (n, LANES))

    units = range(Q_GROUP)
    lows, parts, cums, weights = {}, {}, {}, {}
    for s in range(Q_GROUP + 2):
        if s < Q_GROUP:
            w, lows[s] = masked_logits(s)
        if 0 <= s - 1 < Q_GROUP:
            sp_bf = parts[s - 1][0]
            cums[s - 1] = (suffix(sp_bf[:, :th], tri), suffix(sp_bf[:, th:], tri_diag))
        if 0 <= s - 2 < Q_GROUP:
            first_output(s - 2, lows[s - 2], parts[s - 2], weights[s - 2])
        if s < Q_GROUP:
            parts[s] = softplus_parts(w)
        if 0 <= s - 1 < Q_GROUP:
            weights[s - 1] = first_weights(parts[s - 1], cums[s - 1])

    def sweep(u, hi):
        lo = pl.multiple_of(jnp.maximum(hi - tk, 0), tq)
        w = logits(u, lo, tk)
        col = lax.broadcasted_iota(jnp.int32, (n, tk), 1)
        w = jnp.where(col < hi - lo, jnp.minimum(w, Z_CLAMP), MASKED_LOGIT)
        sp_bf, rs0, rs1, part = softplus_parts(w)
        run = run_ref[u]
        off0 = run + rs1
        a0 = jnp.exp2((part[:, :th] - suffix(sp_bf[:, :th], tri)) - jnp.concatenate([off0, off0], axis=1))
        a1 = jnp.exp2((part[:, th:] - suffix(sp_bf[:, th:], tri)) - jnp.concatenate([run, run], axis=1))
        a = jnp.concatenate([a0, a1], axis=1).astype(BF16)
        vb = v_ref[0, pl.ds(lo, tk), :]
        acc_ref[u] += jnp.dot(a, vb, preferred_element_type=F32)
        run_ref[u] = off0 + rs0

    def first_lo(u):
        return jnp.maximum((j * Q_GROUP + u + 1) * tq - tw, 0)

    def finish_unit(u, carry):
        def more(state):
            hi, min_run = state
            return jnp.logical_and(hi > 0, min_run < RUN_DONE)

        def step(state):
            hi, _ = state
            sweep(u, hi)
            return jnp.maximum(hi - tk, 0), jnp.min(run_ref[u])

        lax.while_loop(more, step, (first_lo(u), jnp.min(run_ref[u])))
        return carry

    pending = None
    for u in units:
        run_u = jnp.where(first_lo(u) > 0, run_ref[u], RUN_DONE)
        pending = run_u if pending is None else jnp.minimum(pending, run_u)

    @pl.when(jnp.min(pending) < RUN_DONE)
    def _():
        lax.fori_loop(0, Q_GROUP, finish_unit, 0)

    for u in units:
        acc = acc_ref[u]
        o_ref[0, u * tq:(u + 1) * tq, :] = jnp.where(first, acc[0:tq], acc[tq:n]).astype(BF16)


def _attention(q, k, v, tri, causal_bias):
    bsz, seq, _ = q.shape
    n_pairs = SB_WIDTH // LANES
    rows = Q_GROUP * Q_UNIT
    q_tile = pl.BlockSpec((1, rows, LANES), lambda b, hp, j: (b, j, hp))
    kv_full = pl.BlockSpec((1, seq, LANES), lambda b, hp, j: (b, 0, hp))
    return pl.pallas_call(
        _attn_kernel,
        grid=(bsz, n_pairs, seq // rows),
        in_specs=[q_tile, kv_full, kv_full, _resident((ATT_TILE, ATT_TILE)), _resident((ATT_TILE, ATT_TILE))],
        out_specs=q_tile,
        out_shape=jax.ShapeDtypeStruct(q.shape, BF16),
        scratch_shapes=[pltpu.VMEM((Q_GROUP, 2 * Q_UNIT, LANES), F32),
                        pltpu.VMEM((Q_GROUP, 2 * Q_UNIT, LANES), F32)],
        compiler_params=_params(3),
        name="sb_attention",
    )(q, k, v, tri, causal_bias)


def _pool_block_diag(pool_w):
    n_layers, g, c, _ = pool_w.shape
    eye = jnp.eye(g, dtype=pool_w.dtype)
    blocks = eye[None, :, None, :, None] * pool_w[:, :, :, None, :]
    return blocks.reshape(n_layers, g * c, g * c).astype(BF16)


def kernel(x, c, w_ada, b_ada, ffn1_norm, ffn1_gate, ffn1_up, ffn1_down, mix_norm, w_in, q_norm,
           k_norm, conv_w, pool_w, pool_scale, w_out, ffn2_norm, ffn2_gate, ffn2_up, ffn2_down):
    n_layers = w_ada.shape[0]
    bsz = x.shape[0]
    mod = _ada_all_layers(c, w_ada, b_ada).reshape(n_layers, bsz, N_MOD, D_MODEL)
    lower = jnp.tril(jnp.ones((ATT_TILE, ATT_TILE), F32), -1)
    tri = lower.astype(BF16)
    causal_bias = (1.0 - lower) * MASKED_LOGIT

    ffn1 = (ffn1_norm[:, None], ffn1_gate.astype(BF16), ffn1_up.astype(BF16), ffn1_down.astype(BF16))
    ffn2 = (ffn2_norm[:, None], ffn2_gate.astype(BF16), ffn2_up.astype(BF16), ffn2_down.astype(BF16))
    mixer = (mix_norm[:, None], w_in.astype(BF16), jnp.tile(q_norm, (1, 2))[:, None],
             jnp.tile(k_norm, (1, 2))[:, None], conv_w, _pool_block_diag(pool_w), pool_scale[:, None])
    w_out = w_out.astype(BF16)

    for l in range(n_layers):
        x, q, k, v, y_cp = _ffn_proj(l, x, mod, *ffn1, *mixer)
        y_sb = _attention(q, k, v, tri, causal_bias)
        x = _ffn(l, x, mod, *ffn2, mod_base=6, mix=(y_sb, y_cp, w_out))
    return x
```
